```python
import math
import jax, jax.numpy as jnp
from jax import lax
import numpy as np


D_MODEL = 1024
BATCH = 16
SEQ = 256
DEPTH = 4
DEC_BATCH = 2
DEC_SEQ = 1024
PAST_LEN = 512

GRID_W = 64
N_MIXERS = 3
N_POOL = (DEPTH + 2) // 3
N_HGRN = (DEPTH + 1) // 3
N_DIFF = DEPTH // 3

POOL_WINDOWS = (2, 4, 8, 16)
POOL_GROUP = D_MODEL // len(POOL_WINDOWS)

HGRN_EXPAND = 128
HGRN_HEADS = D_MODEL // HGRN_EXPAND
HGRN_DK = HGRN_EXPAND
HGRN_DV = D_MODEL // HGRN_HEADS
CHUNK = 32

DIFF_HEAD_DIM = 64
DIFF_HEADS = D_MODEL // (2 * DIFF_HEAD_DIM)
Q_BLOCK = 128
ROPE_BASE = 10000.0

D_FF = ((8 * D_MODEL // 3) + 127) // 128 * 128
CONV_W = 3
EPS = 1e-6

kernel_name = 'hybrid_diffusion_pool_hgrn2_diffattn_step'

F32 = jnp.float32


def rms_norm(x, g):
    xf = x.astype(F32)
    y = xf * lax.rsqrt(jnp.mean(xf * xf, axis=-1, keepdims=True) + EPS)
    return (y * g.astype(F32)).astype(x.dtype)


def modulation(cond, w, b, dtype):
    m = jax.nn.silu(cond.astype(F32)) @ w.astype(F32) + b.astype(F32)
    return [a[:, None, :].astype(dtype) for a in jnp.split(m, 6, axis=-1)]


def box_mean_1d(x, w, axis):
    n = x.shape[axis]
    cs = jnp.cumsum(x.astype(F32), axis=axis)
    cs = jnp.concatenate([jnp.zeros_like(lax.slice_in_dim(cs, 0, 1, axis=axis)), cs], axis=axis)
    t = jnp.arange(n)
    lo = jnp.clip(t - w // 2, 0, n)
    hi = jnp.clip(t + w // 2, 0, n)
    s = jnp.take(cs, hi, axis=axis) - jnp.take(cs, lo, axis=axis)
    shape = [1] * x.ndim
    shape[axis] = n
    return s / (hi - lo).astype(F32).reshape(shape)


def pool_mixer(h, rows, cols, w_pool, scale):
    bsz, n, _ = h.shape
    hg = h.reshape(bsz, rows, cols, len(POOL_WINDOWS), POOL_GROUP)
    outs = []
    for g, w in enumerate(POOL_WINDOWS):
        xg = hg[..., g, :]
        m = box_mean_1d(box_mean_1d(xg, w, 1), w, 2)
        outs.append(m - xg.astype(F32))
    d = jnp.stack(outs, axis=3)
    y = jnp.einsum('brwgi,gio->brwgo', d, w_pool.astype(F32))
    return (y.reshape(bsz, n, D_MODEL) * scale.astype(F32)).astype(h.dtype)


def conv_ffn(h, rows, cols, w_up, conv_w, conv_b, w_down):
    bsz, n, _ = h.shape
    u = (h @ w_up).reshape(bsz, rows, cols, 2 * D_FF)
    u = lax.conv_general_dilated(u, conv_w[:, :, None, :].astype(u.dtype), window_strides=(1, 1),
                                 padding='SAME', dimension_numbers=('NHWC', 'HWIO', 'NHWC'),
                                 feature_group_count=2 * D_FF) + conv_b
    a, v = jnp.split(u, 2, axis=-1)
    return (jax.nn.silu(a) * v).reshape(bsz, n, D_FF) @ w_down


def gla_scan(q, k, v, logf, s0):
    bsz, n, nh, _ = q.shape
    nc = n // CHUNK

    def chunks(a):
        return a.reshape(bsz, nc, CHUNK, nh, a.shape[-1]).transpose(1, 0, 3, 2, 4)

    lower = jnp.tril(jnp.ones((CHUNK, CHUNK), dtype=bool))[:, :, None]

    def step(state, inp):
        qc, kc, vc, fc = inp
        cum = jnp.cumsum(fc, axis=2)
        rel = cum[:, :, :, None, :] - cum[:, :, None, :, :]
        decay = jnp.exp(jnp.where(lower, rel, -jnp.inf))
        scores = jnp.einsum('bhtc,bhsc,bhtsc->bhts', qc, kc, decay)
        out = (jnp.einsum('bhts,bhsv->bhtv', scores, vc)
               + jnp.einsum('bhtc,bhcv->bhtv', qc * jnp.exp(cum), state))
        last = cum[:, :, -1:, :]
        state = (jnp.exp(last[:, :, 0, :])[..., None] * state
                 + jnp.einsum('bhsc,bhsv->bhcv', kc * jnp.exp(last - cum), vc))
        return state, out

    state, out = lax.scan(step, s0, (chunks(q), chunks(k), chunks(v), chunks(logf)))
    return out.transpose(1, 0, 3, 2, 4).reshape(bsz, n, nh, v.shape[-1]), state


def hgrn_mixer(h, w_in, lb_param, layer_idx, norm_g, w_out, s0):
    bsz, n, _ = h.shape
    q, iv, g, zf, zb = jnp.split((h @ w_in).astype(F32), 5, axis=-1)
    lbs = jax.nn.softmax(lb_param.astype(F32), axis=1)
    lbs = jnp.cumsum(lbs, axis=1) - lbs[:, :1]
    lb = lbs[:, layer_idx].reshape(2, HGRN_HEADS, HGRN_DK)

    def heads(a):
        return a.reshape(bsz, n, HGRN_HEADS, -1)

    q, iv, g = heads(q), heads(iv), heads(g)
    f_fw = lb[0] + (1.0 - lb[0]) * jax.nn.sigmoid(heads(zf))
    f_bw = lb[1] + (1.0 - lb[1]) * jax.nn.sigmoid(heads(zb))
    s0 = s0.astype(F32)
    o_fw, s_fw = gla_scan(q, 1.0 - f_fw, iv, jnp.log(f_fw), s0[:, 0])

    def rev(a):
        return a[:, ::-1]

    o_bw, s_bw = gla_scan(rev(q), rev(1.0 - f_bw), rev(iv), rev(jnp.log(f_bw)), s0[:, 1])
    o = rms_norm(o_fw + rev(o_bw), norm_g) * jax.nn.silu(g)
    y = o.reshape(bsz, n, D_MODEL) @ w_out.astype(F32)
    return y.astype(h.dtype), jnp.stack([s_fw, s_bw], axis=1)


def axial_rope(x):
    n = x.shape[1]
    t = jnp.arange(n)
    half = DIFF_HEAD_DIM // 2
    nf = half // 2
    inv = ROPE_BASE ** (-jnp.arange(nf, dtype=F32) / nf)

    def rot(xa, pos):
        ang = pos.astype(F32)[:, None] * inv
        cos = jnp.cos(ang)[None, :, None, None, :]
        sin = jnp.sin(ang)[None, :, None, None, :]
        x1, x2 = jnp.split(xa.astype(F32), 2, axis=-1)
        return jnp.concatenate([x1 * cos - x2 * sin, x2 * cos + x1 * sin], axis=-1)

    out = jnp.concatenate([rot(x[..., :half], t // GRID_W), rot(x[..., half:], t % GRID_W)], axis=-1)
    return out.astype(x.dtype)


def diff_attention(q, k, v, lam):
    bsz, nq = q.shape[:2]
    nb = nq // Q_BLOCK
    qb = q.astype(F32).reshape(bsz, nb, Q_BLOCK, DIFF_HEADS, 2, DIFF_HEAD_DIM).transpose(1, 0, 2, 3, 4, 5)
    kf, vf = k.astype(F32), v.astype(F32)
    scale = DIFF_HEAD_DIM ** -0.5

    def block(qi):
        p = jax.nn.softmax(jnp.einsum('bqhcd,bkhcd->bhcqk', qi, kf) * scale, axis=-1)
        w = p[:, :, 0] - lam * p[:, :, 1]
        return jnp.einsum('bhqk,bkhe->bqhe', w, vf)

    out = lax.map(block, qb)
    return out.transpose(1, 0, 2, 3, 4).reshape(bsz, nq, DIFF_HEADS, 2 * DIFF_HEAD_DIM)


def trunk_layer(x, cond, i, p, rows, cols, cache):
    sh1, sc1, gt1, sh2, sc2, gt2 = modulation(cond, p['ada_w'][i], p['ada_b'][i], x.dtype)
    h = rms_norm(x, p['norm1'][i]) * (1 + sc1) + sh1
    kind, j = i % N_MIXERS, i // N_MIXERS
    produced = None
    bsz, n, _ = x.shape
    if kind == 0:
        y = pool_mixer(h, rows, cols, p['pool_w'][j], p['pool_scale'][j])
    elif kind == 1:
        if cache is None:
            s0 = jnp.zeros((bsz, 2, HGRN_HEADS, HGRN_DK, HGRN_DV), F32)
        else:
            s0 = cache['state_hgrn'][:, j]
        y, produced = hgrn_mixer(h, p['hgrn_w_in'][j], p['hgrn_lb'], i, p['hgrn_norm'][j],
                                 p['hgrn_w_out'][j], s0)
    else:
        q, k, v = jnp.split(h @ p['diff_w_in'][j], 3, axis=-1)
        q = q.reshape(bsz, n, DIFF_HEADS, 2, DIFF_HEAD_DIM)
        k = k.reshape(bsz, n, DIFF_HEADS, 2, DIFF_HEAD_DIM)
        v = v.reshape(bsz, n, DIFF_HEADS, 2 * DIFF_HEAD_DIM)
        if cache is None:
            produced = (k, v)
        else:
            q, k = axial_rope(q), axial_rope(k)
            k = jnp.concatenate([cache['cache_k'][:, j].astype(k.dtype), k], axis=1)
            v = jnp.concatenate([cache['cache_v'][:, j].astype(v.dtype), v], axis=1)
        lam_init = 0.8 - 0.6 * math.exp(-0.3 * i)
        lp = p['diff_lambda'][j].astype(F32)
        lam = jnp.exp(jnp.sum(lp[0] * lp[1])) - jnp.exp(jnp.sum(lp[2] * lp[3])) + lam_init
        o = diff_attention(q, k, v, lam)
        o = rms_norm(o, p['diff_subln'][j]) * (1.0 - lam_init)
        y = o.reshape(bsz, n, D_MODEL) @ p['diff_w_out'][j].astype(F32)
    x = x + gt1 * y.astype(x.dtype)
    h = rms_norm(x, p['norm2'][i]) * (1 + sc2) + sh2
    f = conv_ffn(h, rows, cols, p['ffn_w_up'][i], p['ffn_conv_w'][i], p['ffn_conv_b'][i], p['ffn_w_down'][i])
    x = x + gt2 * f.astype(x.dtype)
    return x, produced


def setup_inputs(seed: int = 0) -> dict:
    key = jax.random.key(seed)
    ks = jax.random.split(key, 32)
    nrm = jax.random.normal
    d = D_MODEL
    return {
        'x_prompt': nrm(ks[0], (BATCH, SEQ, d), F32),
        'x_sample': nrm(ks[1], (DEC_BATCH, DEC_SEQ, d), F32),
        'state_hgrn': nrm(ks[2], (DEC_BATCH, N_HGRN, 2, HGRN_HEADS, HGRN_DK, HGRN_DV), F32),
        'cache_k': nrm(ks[3], (DEC_BATCH, N_DIFF, PAST_LEN, DIFF_HEADS, 2, DIFF_HEAD_DIM), F32),
        'cache_v': nrm(ks[4], (DEC_BATCH, N_DIFF, PAST_LEN, DIFF_HEADS, 2 * DIFF_HEAD_DIM), F32),
        'c': nrm(ks[5], (DEC_BATCH, d), F32),
        'c_ctx': nrm(ks[6], (d,), F32),
        'ada_w': nrm(ks[7], (DEPTH, d, 6 * d), F32) * (0.5 * d ** -0.5),
        'ada_b': nrm(ks[8], (DEPTH, 6 * d), F32) * 0.01,
        'norm1': 1.0 + 0.02 * nrm(ks[9], (DEPTH, d), F32),
        'norm2': 1.0 + 0.02 * nrm(ks[10], (DEPTH, d), F32),
        'final_norm': 1.0 + 0.02 * nrm(ks[11], (d,), F32),
        'pool_w': nrm(ks[12], (N_POOL, len(POOL_WINDOWS), POOL_GROUP, POOL_GROUP), F32) * POOL_GROUP ** -0.5,
        'pool_scale': 1.0 + 0.02 * nrm(ks[13], (N_POOL, d), F32),
        'hgrn_w_in': nrm(ks[14], (N_HGRN, d, 5 * d), F32) * d ** -0.5,
        'hgrn_lb': nrm(ks[15], (2, DEPTH, d), F32) * 0.5,
        'hgrn_norm': 1.0 + 0.02 * nrm(ks[16], (N_HGRN, HGRN_DV), F32),
        'hgrn_w_out': nrm(ks[17], (N_HGRN, d, d), F32) * d ** -0.5,
        'diff_w_in': nrm(ks[18], (N_DIFF, d, 3 * d), F32) * d ** -0.5,
        'diff_lambda': nrm(ks[19], (N_DIFF, 4, DIFF_HEAD_DIM), F32) * 0.1,
        'diff_subln': 1.0 + 0.02 * nrm(ks[20], (N_DIFF, 2 * DIFF_HEAD_DIM), F32),
        'diff_w_out': nrm(ks[21], (N_DIFF, d, d), F32) * d ** -0.5,
        'ffn_w_up': nrm(ks[22], (DEPTH, d, 2 * D_FF), F32) * d ** -0.5,
        'ffn_conv_w': nrm(ks[23], (DEPTH, CONV_W, CONV_W, 2 * D_FF), F32) * (1.0 / CONV_W),
        'ffn_conv_b': nrm(ks[24], (DEPTH, 2 * D_FF), F32) * 0.01,
        'ffn_w_down': nrm(ks[25], (DEPTH, D_FF, d), F32) * D_FF ** -0.5,
    }


def reference(x_prompt, x_sample, state_hgrn, cache_k, cache_v, c, c_ctx, ada_w, ada_b, norm1, norm2,
              final_norm, pool_w, pool_scale, hgrn_w_in, hgrn_lb, hgrn_norm, hgrn_w_out, diff_w_in,
              diff_lambda, diff_subln, diff_w_out, ffn_w_up, ffn_conv_w, ffn_conv_b, ffn_w_down):
    p = dict(ada_w=ada_w, ada_b=ada_b, norm1=norm1, norm2=norm2, pool_w=pool_w, pool_scale=pool_scale,
             hgrn_w_in=hgrn_w_in, hgrn_lb=hgrn_lb, hgrn_norm=hgrn_norm, hgrn_w_out=hgrn_w_out,
             diff_w_in=diff_w_in, diff_lambda=diff_lambda, diff_subln=diff_subln, diff_w_out=diff_w_out,
             ffn_w_up=ffn_w_up, ffn_conv_w=ffn_conv_w, ffn_conv_b=ffn_conv_b, ffn_w_down=ffn_w_down)
    cache = dict(state_hgrn=state_hgrn, cache_k=cache_k, cache_v=cache_v)
    ctx_cond = c_ctx[None, :]
    rows = x_sample.shape[1] // GRID_W
    xp, xs = x_prompt, x_sample
    states, keys, vals = [], [], []
    for i in range(DEPTH):
        xp, produced = trunk_layer(xp, ctx_cond, i, p, 1, xp.shape[1], None)
        if i % N_MIXERS == 1:
            states.append(produced)
        elif i % N_MIXERS == 2:
            keys.append(produced[0])
            vals.append(produced[1])
        xs, _ = trunk_layer(xs, c, i, p, rows, GRID_W, cache)
    y_prompt = rms_norm(xp, final_norm)
    y_sample = rms_norm(xs, final_norm)
    new_state_hgrn = jnp.stack(states, axis=1)
    new_cache_k = jnp.stack(keys, axis=1)
    new_cache_v = jnp.stack(vals, axis=1)
    return (y_prompt, y_sample, new_state_hgrn, new_cache_k, new_cache_v)
```

```python
import functools
import math

import numpy as np
import jax
import jax.numpy as jnp
from jax import lax
from jax.experimental import pallas as pl
from jax.experimental.pallas import tpu as pltpu

F32 = jnp.float32
BF = jnp.bfloat16

D = 1024
DEPTH = 4
GRID_W = 64
POOL_WINDOWS = (2, 4, 8, 16)
POOL_GROUP = D // len(POOL_WINDOWS)
HEADS = 8
HD = 128
DIFF_DH = 64
ROPE_BASE = 10000.0
D_FF = 2816
EPS = 1e-6

LANES = 128
SUBLANES = 8
TOK = 1024
FF_TILE = 256
N_FF_TILES = D_FF // FF_TILE
CHUNK = 128
ATT_QB = 256
MOD_TN = 1536
MOD_LC = 768
VMEM_LIMIT = 56 * 1024 * 1024


def _cp(sem, vmem=VMEM_LIMIT):
    return pltpu.CompilerParams(dimension_semantics=sem, vmem_limit_bytes=vmem)


def _sigmoid(x):
    return 1.0 / (1.0 + jnp.exp(-x))


def _rms(x, g):
    return x * lax.rsqrt(jnp.mean(x * x, axis=-1, keepdims=True) + EPS) * g


def _dot(a, b):
    return jnp.dot(a, b, preferred_element_type=F32)


def _dot_nt(a, b):
    return lax.dot_general(a, b, (((1,), (1,)), ((), ())), preferred_element_type=F32)


def _dot_tn(a, b):
    return lax.dot_general(a, b, (((0,), (0,)), ((), ())), preferred_element_type=F32)


def _mod_kernel(ct_ref, w_ref, b_ref, o_ref, sb_ref):
    @pl.when((pl.program_id(0) == 0) & (pl.program_id(1) == 0))
    def _():
        ct = ct_ref[...]
        s = ct * _sigmoid(ct)
        for r in range(3):
            sb_ref[r] = jnp.broadcast_to(s[:, r:r + 1], (D, LANES))

    for lc in range(MOD_TN // MOD_LC):
        c0 = lc * MOD_LC

        def body(i, accs, c0=c0):
            k0 = pl.multiple_of(i * SUBLANES, SUBLANES)
            wv = w_ref[pl.ds(k0, SUBLANES), c0:c0 + MOD_LC]
            out = []
            for r in range(3):
                sv = sb_ref[r, pl.ds(k0, SUBLANES), :]
                out.append(accs[r] + wv * jnp.tile(sv, (1, MOD_LC // LANES)))
            return tuple(out)

        accs = lax.fori_loop(0, D // SUBLANES, body,
                             tuple(jnp.zeros((SUBLANES, MOD_LC), F32) for _ in range(3)), unroll=4)
        for r in range(3):
            o_ref[r:r + 1, c0:c0 + MOD_LC] = (jnp.sum(accs[r], axis=0, keepdims=True)
                                              + b_ref[:, c0:c0 + MOD_LC])
        o_ref[3:8, c0:c0 + MOD_LC] = jnp.zeros((5, MOD_LC), F32)


def _modulation(cond_t, ada_w, ada_b):
    return pl.pallas_call(
        _mod_kernel,
        grid=(DEPTH, 6 * D // MOD_TN),
        in_specs=[pl.BlockSpec((D, 8), lambda l, j: (0, 0)),
                  pl.BlockSpec((None, D, MOD_TN), lambda l, j: (l, 0, j)),
                  pl.BlockSpec((None, 1, MOD_TN), lambda l, j: (l, 0, j))],
        out_specs=pl.BlockSpec((None, 8, MOD_TN), lambda l, j: (l, 0, j)),
        out_shape=jax.ShapeDtypeStruct((DEPTH, 8, 6 * D), F32),
        scratch_shapes=[pltpu.VMEM((3, D, LANES), F32)],
        compiler_params=_cp(("arbitrary", "arbitrary")),
        name="modulation",
    )(cond_t, ada_w, ada_b.reshape(DEPTH, 1, 6 * D))


def _mod_spec(layer, row0, row_step):
    return pl.BlockSpec((None, None, 6, D), lambda i, *_: (layer, row0 + row_step * i, 0, 0))


def _pool_kernel(x_ref, m_ref, g_ref, p_ref, ic_ref, w_ref, sc_ref, o_ref):
    x = x_ref[...]
    m = m_ref[...]
    h = _rms(x, g_ref[...]) * (1.0 + m[1:2]) + m[0:1]
    for g in range(len(POOL_WINDOWS)):
        sl = slice(g * POOL_GROUP, (g + 1) * POOL_GROUP)
        hg = h[:, sl]
        h1 = hg.astype(BF)
        r1 = hg - h1.astype(F32)
        h2 = r1.astype(BF)
        h3 = (r1 - h2.astype(F32)).astype(BF)
        p = p_ref[g]
        s = _dot(p, h1) + _dot(p, h2) + _dot(p, h3)
        ic = ic_ref[g]
        d = s * jnp.concatenate([ic, ic], axis=1) - hg
        y = _dot(d.astype(BF), w_ref[g]) * sc_ref[:, sl]
        o_ref[:, sl] = x[:, sl] + m[2:3, sl] * y


def _pool_tables(rows, cols, segs):
    n = rows * cols
    r = np.arange(n) // cols
    c = np.arange(n) % cols
    mats, invs = [], []
    for w in POOL_WINDOWS:
        def memb(pos, size):
            lo = np.clip(pos - w // 2, 0, size)
            hi = np.clip(pos + w // 2, 0, size)
            j = np.arange(size)
            return ((j[None, :] >= lo[:, None]) & (j[None, :] < hi[:, None])), (hi - lo)
        mr, cr = memb(np.arange(rows), rows)
        mc, cc = memb(np.arange(cols), cols)
        p = mr[r][:, r] & mc[c][:, c]
        cnt = (cr[r] * cc[c]).astype(np.float64)
        mats.append(np.kron(np.eye(segs), p.astype(np.float64)))
        invs.append(np.tile(1.0 / cnt, segs))
    p01 = jnp.asarray(np.stack(mats), dtype=BF)
    inv = jnp.asarray(np.broadcast_to(np.stack(invs)[:, :, None], (len(POOL_WINDOWS), n * segs, LANES)),
                      dtype=F32)
    return p01, inv


def _pool_layer(x, mods, layer, row0, row_step, g1, p01, inv, w, scale):
    nblk = x.shape[0] // TOK
    const3 = lambda i: (0, 0, 0)
    return pl.pallas_call(
        _pool_kernel,
        grid=(nblk,),
        in_specs=[pl.BlockSpec((TOK, D), lambda i: (i, 0)),
                  _mod_spec(layer, row0, row_step),
                  pl.BlockSpec((1, D), lambda i: (0, 0)),
                  pl.BlockSpec((4, TOK, TOK), const3),
                  pl.BlockSpec((4, TOK, LANES), const3),
                  pl.BlockSpec((4, POOL_GROUP, POOL_GROUP), const3),
                  pl.BlockSpec((1, D), lambda i: (0, 0))],
        out_specs=pl.BlockSpec((TOK, D), lambda i: (i, 0)),
        out_shape=jax.ShapeDtypeStruct(x.shape, F32),
        compiler_params=_cp(("parallel",)),
        name="pool_mixer",
    )(x, mods, g1, p01, inv, w, scale)


def _rope_tile(y, cos, sa, sb):
    outs = []
    for c in range(D // LANES):
        yc = y[:, c * LANES:(c + 1) * LANES]
        up = pltpu.roll(yc, LANES - 16, axis=1)
        dn = pltpu.roll(yc, 16, axis=1)
        outs.append(yc * cos + up * sa + dn * sb)
    return jnp.concatenate(outs, axis=1)


def _inproj_kernel(x_ref, m_ref, g_ref, w_ref, *rest, rope):
    if rope:
        cos_ref, sa_ref, sb_ref, o_ref, h_sc = rest
    else:
        o_ref, h_sc = rest
    j = pl.program_id(1)

    @pl.when(j == 0)
    def _():
        m = m_ref[...]
        h = _rms(x_ref[...], g_ref[...]) * (1.0 + m[1:2]) + m[0:1]
        h_sc[...] = h.astype(BF)

    y = _dot(h_sc[...], w_ref[...])
    if rope:
        @pl.when(j < 2)
        def _():
            o_ref[...] = _rope_tile(y, cos_ref[...], sa_ref[...], sb_ref[...])

        @pl.when(j >= 2)
        def _():
            o_ref[...] = y
    else:
        o_ref[...] = y


def _inproj(x, mods, layer, row0, row_step, g1, w_bf, rope_tabs=None):
    t = x.shape[0]
    nblk = t // TOK
    ntile = w_bf.shape[1] // D
    in_specs = [pl.BlockSpec((TOK, D), lambda i, j: (i, 0)),
                _mod_spec(layer, row0, row_step),
                pl.BlockSpec((1, D), lambda i, j: (0, 0)),
                pl.BlockSpec((D, D), lambda i, j: (0, j))]
    args = [x, mods, g1, w_bf]
    if rope_tabs is not None:
        in_specs += [pl.BlockSpec((TOK, LANES), lambda i, j: (0, 0))] * 3
        args += list(rope_tabs)
    return pl.pallas_call(
        functools.partial(_inproj_kernel, rope=rope_tabs is not None),
        grid=(nblk, ntile),
        in_specs=in_specs,
        out_specs=pl.BlockSpec((None, TOK, D), lambda i, j: (j, i, 0)),
        out_shape=jax.ShapeDtypeStruct((ntile, t, D), F32),
        scratch_shapes=[pltpu.VMEM((TOK, D), BF)],
        compiler_params=_cp(("parallel", "arbitrary")),
        name="in_proj",
    )(*args)


def _rope_tables():
    t = np.arange(TOK)
    half = DIFF_DH // 2
    nf = half // 2
    inv = jnp.asarray(ROPE_BASE, F32) ** (-jnp.arange(nf, dtype=F32) / nf)
    lane = np.arange(LANES) % DIFF_DH
    use_col = lane >= half
    fidx = (lane % half) % nf
    second = (lane % half) >= nf
    pos = jnp.where(jnp.asarray(use_col)[None, :], jnp.asarray(t % GRID_W, F32)[:, None],
                    jnp.asarray(t // GRID_W, F32)[:, None])
    ang = pos * inv[jnp.asarray(fidx)][None, :]
    cos, sin = jnp.cos(ang), jnp.sin(ang)
    sec = jnp.asarray(second)[None, :]
    sa = jnp.where(sec, 0.0, -sin)
    sb = jnp.where(sec, sin, 0.0)
    return cos, sa, sb


def _outproj_kernel(a_ref, w_ref, x_ref, m_ref, o_ref):
    y = _dot(a_ref[...].astype(BF), w_ref[...])
    o_ref[...] = x_ref[...] + m_ref[2:3, :] * y


def _outproj(a, w_bf, x, mods, layer, row0, row_step):
    nblk = x.shape[0] // TOK
    return pl.pallas_call(
        _outproj_kernel,
        grid=(nblk,),
        in_specs=[pl.BlockSpec((TOK, D), lambda i: (i, 0)),
                  pl.BlockSpec((D, D), lambda i: (0, 0)),
                  pl.BlockSpec((TOK, D), lambda i: (i, 0)),
                  _mod_spec(layer, row0, row_step)],
        out_specs=pl.BlockSpec((TOK, D), lambda i: (i, 0)),
        out_shape=jax.ShapeDtypeStruct(x.shape, F32),
        compiler_params=_cp(("parallel",)),
        name="out_proj",
    )(a, w_bf, x, mods)


HGRN_LEVELS = (1, 2, 4, 8, 16, 32, 64)


def _hgrn_masks():
    t = np.arange(CHUNK)[:, None]
    s = np.arange(CHUNK)[None, :]
    fwd = [(t == s)]
    for b in HGRN_LEVELS:
        fwd.append((t // (2 * b) == s // (2 * b)) & (t % (2 * b) >= b) & (s % (2 * b) < b))
    fwd = np.stack(fwd).astype(np.float32)
    bwd = np.transpose(fwd, (0, 2, 1))
    tri = np.stack([(s <= t), (s >= t)]).astype(np.float32)
    return jnp.asarray(fwd), jnp.asarray(bwd), jnp.asarray(tri)


def _hgrn_chunk(q, v, z, lb, tri, masks, st, rev):
    f = lb + (1.0 - lb) * _sigmoid(z)
    lf = jnp.log(f)
    k = 1.0 - f
    cum = jnp.dot(tri, lf, preferred_element_type=F32, precision=lax.Precision.HIGHEST)
    tot = cum[0:1] if rev else cum[CHUNK - 1:CHUNK]
    qb = q.astype(BF)

    s = masks[0] * _dot_nt(qb, k.astype(BF))

    pos = lax.broadcasted_iota(jnp.int32, (CHUNK, HD), 0)
    odd = (pos & 1) == 1
    e = jnp.where(odd, 1.0, f) if rev else jnp.where(odd, f, 1.0)
    s = s + masks[1] * _dot_nt((q * e).astype(BF), (k * e).astype(BF))

    lfn = pltpu.roll(lf, CHUNK - 1, axis=0)
    lfp = pltpu.roll(lf, 1, axis=0)
    p4 = pos & 3
    if rev:
        ex = jnp.where(p4 == 0, lf + lfn, jnp.where(p4 == 1, lf, jnp.where(p4 == 2, 0.0, lfp)))
    else:
        ex = jnp.where(p4 == 0, lfn, jnp.where(p4 == 1, 0.0, jnp.where(p4 == 2, lf, lf + lfp)))
    e = jnp.exp(ex)
    s = s + masks[2] * _dot_nt((q * e).astype(BF), (k * e).astype(BF))

    for li, b in enumerate(HGRN_LEVELS[2:]):
        c3 = cum.reshape(CHUNK // (2 * b), 2 * b, HD)
        ridx = b if rev else b - 1
        e = jnp.exp(-jnp.abs(c3 - c3[:, ridx:ridx + 1, :])).reshape(CHUNK, HD)
        s = s + masks[3 + li] * _dot_nt((q * e).astype(BF), (k * e).astype(BF))

    vb = v.astype(BF)
    qh = (q * jnp.exp(cum)).astype(BF)
    kh = (k * jnp.exp(tot - cum)).astype(BF)
    o = _dot(s.astype(BF), vb) + _dot_nt(qh, st.astype(BF))
    st_new = st * jnp.exp(tot) + _dot_tn(vb, kh)
    return o, st_new


def _hgrn_kernel(*refs, n_chunks, has_s0, emit_state):
    q_ref, v_ref, g_ref, zf_ref, zb_ref, lb_ref, ng_ref, mf_ref, mb_ref, tri_ref = refs[:10]
    rest = refs[10:]
    if has_s0:
        s0_ref, rest = rest[0], rest[1:]
    o_ref, rest = rest[0], rest[1:]
    if emit_state:
        st_ref, rest = rest[0], rest[1:]
    of_sc, ob_sc = rest

    lbf = lb_ref[0:1, :]
    lbb = lb_ref[1:2, :]
    if has_s0:
        st0 = (s0_ref[0].T, s0_ref[1].T)
    else:
        st0 = (jnp.zeros((HD, HD), F32), jnp.zeros((HD, HD), F32))

    def body(i, carry):
        stf, stb = carry
        cf = pl.multiple_of(i * CHUNK, CHUNK)
        cb = pl.multiple_of((n_chunks - 1 - i) * CHUNK, CHUNK)
        o_f, stf = _hgrn_chunk(q_ref[pl.ds(cf, CHUNK), :], v_ref[pl.ds(cf, CHUNK), :],
                               zf_ref[pl.ds(cf, CHUNK), :], lbf, tri_ref[0], mf_ref, stf, False)
        o_b, stb = _hgrn_chunk(q_ref[pl.ds(cb, CHUNK), :], v_ref[pl.ds(cb, CHUNK), :],
                               zb_ref[pl.ds(cb, CHUNK), :], lbb, tri_ref[1], mb_ref, stb, True)
        of_sc[pl.ds(cf, CHUNK), :] = o_f
        ob_sc[pl.ds(cb, CHUNK), :] = o_b
        return stf, stb

    stf, stb = lax.fori_loop(0, n_chunks, body, st0)
    if emit_state:
        st_ref[0] = stf.T
        st_ref[1] = stb.T
    o = of_sc[...] + ob_sc[...]
    g = g_ref[...]
    o_ref[...] = _rms(o, ng_ref[...]) * (g * _sigmoid(g))


def _hgrn_scan(proj, n_seq, lb, ng, consts, s0=None, emit_state=False):
    t = proj.shape[1]
    nb = t // n_seq
    mf, mb, tri = consts

    def pspec(k):
        return pl.BlockSpec((None, n_seq, HD), lambda b, h, k=k: (k, b, h))

    c3 = lambda b, h: (0, 0, 0)
    in_specs = [pspec(0), pspec(1), pspec(2), pspec(3), pspec(4),
                pl.BlockSpec((None, 2, HD), lambda b, h: (h, 0, 0)),
                pl.BlockSpec((1, HD), lambda b, h: (0, 0)),
                pl.BlockSpec(mf.shape, c3), pl.BlockSpec(mb.shape, c3), pl.BlockSpec(tri.shape, c3)]
    args = [proj, proj, proj, proj, proj, lb, ng, mf, mb, tri]
    if s0 is not None:
        in_specs.append(pl.BlockSpec((None, 2, None, HD, HD), lambda b, h: (b, 0, h, 0, 0)))
        args.append(s0)
    out_specs = [pl.BlockSpec((n_seq, HD), lambda b, h: (b, h))]
    out_shape = [jax.ShapeDtypeStruct((t, D), F32)]
    if emit_state:
        out_specs.append(pl.BlockSpec((None, 2, None, HD, HD), lambda b, h: (b, 0, h, 0, 0)))
        out_shape.append(jax.ShapeDtypeStruct((nb, 2, HEADS, HD, HD), F32))
    res = pl.pallas_call(
        functools.partial(_hgrn_kernel, n_chunks=n_seq // CHUNK, has_s0=s0 is not None, emit_state=emit_state),
        grid=(nb, HEADS),
        in_specs=in_specs,
        out_specs=out_specs,
        out_shape=out_shape,
        scratch_shapes=[pltpu.VMEM((n_seq, HD), F32), pltpu.VMEM((n_seq, HD), F32)],
        compiler_params=_cp(("parallel", "parallel")),
        name="hgrn_scan",
    )(*args)
    return res


def _attn_kernel(*refs, lam_init, has_cache):
    lam_ref, q_ref = refs[:2]
    if has_cache:
        kc_ref, vc_ref, kn_ref, vn_ref, sg_ref, o_ref = refs[2:]
        k_refs, v_refs = (kc_ref, kn_ref), (vc_ref, vn_ref)
    else:
        kn_ref, vn_ref, sg_ref, o_ref = refs[2:]
        k_refs, v_refs = (kn_ref,), (vn_ref,)

    lp = lam_ref[...]
    lam = (jnp.exp(jnp.sum(lp[0:1] * lp[1:2], axis=-1, keepdims=True))
           - jnp.exp(jnp.sum(lp[2:3] * lp[3:4], axis=-1, keepdims=True)) + lam_init)

    q = q_ref[...] * (DIFF_DH ** -0.5)
    first = lax.broadcasted_iota(jnp.int32, (1, HD), 1) < DIFF_DH
    qs = (jnp.where(first, q, 0.0).astype(BF), jnp.where(first, 0.0, q).astype(BF))
    ks = [r[...].astype(BF) for r in k_refs]
    vs = [r[...].astype(BF) for r in v_refs]

    outs = []
    for qm in qs:
        sc = [_dot_nt(qm, kk) for kk in ks]
        mx = functools.reduce(jnp.maximum, [jnp.max(s, axis=-1, keepdims=True) for s in sc])
        es = [jnp.exp(s - mx) for s in sc]
        den = functools.reduce(jnp.add, [jnp.sum(e, axis=-1, keepdims=True) for e in es])
        pv = functools.reduce(jnp.add, [_dot(e.astype(BF), vv) for e, vv in zip(es, vs)])
        outs.append(pv * (1.0 / den))
    o = outs[0] - lam * outs[1]
    o_ref[...] = _rms(o, sg_ref[...]) * (1.0 - lam_init)


def _attention(qkv, n_seq, lam_p, subln, lam_init, cache=None):
    t = qkv.shape[1]
    nb = t // n_seq
    nqb = n_seq // ATT_QB
    in_specs = [pl.BlockSpec((4, DIFF_DH), lambda b, h, i: (0, 0)),
                pl.BlockSpec((None, ATT_QB, HD), lambda b, h, i: (0, b * nqb + i, h))]
    args = [lam_p, qkv]
    if cache is not None:
        past = cache[0].shape[1]
        in_specs += [pl.BlockSpec((None, past, HD), lambda b, h, i: (b, 0, h))] * 2
        args += [cache[0], cache[1]]
    in_specs += [pl.BlockSpec((None, n_seq, HD), lambda b, h, i: (1, b, h)),
                 pl.BlockSpec((None, n_seq, HD), lambda b, h, i: (2, b, h)),
                 pl.BlockSpec((1, HD), lambda b, h, i: (0, 0))]
    args += [qkv, qkv, subln]
    return pl.pallas_call(
        functools.partial(_attn_kernel, lam_init=lam_init, has_cache=cache is not None),
        grid=(nb, HEADS, nqb),
        in_specs=in_specs,
        out_specs=pl.BlockSpec((ATT_QB, HD), lambda b, h, i: (b * nqb + i, h)),
        out_shape=jax.ShapeDtypeStruct((t, D), F32),
        compiler_params=_cp(("parallel", "parallel", "arbitrary")),
        name="diff_attention",
    )(*args)


def _shift_rows(a, n):
    z = jnp.zeros((abs(n), a.shape[1]), a.dtype)
    if n > 0:
        return jnp.concatenate([z, a[:-n]], axis=0)
    return jnp.concatenate([a[-n:], z], axis=0)


def _ffn_kernel(*refs, two_d, final):
    x_ref, m_ref, g_ref, wup_ref, cw_ref, cb_ref, wdn_ref, ml_ref, mr_ref = refs[:9]
    rest = refs[9:]
    if final:
        fg_ref, rest = rest[0], rest[1:]
    o_ref, h_sc, acc_sc = rest
    j = pl.program_id(1)

    @pl.when(j == 0)
    def _():
        m = m_ref[...]
        h = _rms(x_ref[...], g_ref[...]) * (1.0 + m[4:5]) + m[3:4]
        h_sc[...] = h.astype(BF)
        acc_sc[...] = jnp.zeros_like(acc_sc)

    u = _dot(h_sc[...], wup_ref[...])
    cw = cw_ref[...]
    reps = 2 * FF_TILE // LANES
    ml = jnp.tile(ml_ref[...], (1, reps))
    mr = jnp.tile(mr_ref[...], (1, reps))
    ul = pltpu.roll(u, 1, axis=0)
    ur = pltpu.roll(u, TOK - 1, axis=0)
    if two_d:
        left = cw[0:1] * _shift_rows(ul, GRID_W) + cw[3:4] * ul + cw[6:7] * _shift_rows(ul, -GRID_W)
        mid = cw[1:2] * _shift_rows(u, GRID_W) + cw[4:5] * u + cw[7:8] * _shift_rows(u, -GRID_W)
        right = cw[2:3] * _shift_rows(ur, GRID_W) + cw[5:6] * ur + cw[8:9] * _shift_rows(ur, -GRID_W)
    else:
        left, mid, right = cw[3:4] * ul, cw[4:5] * u, cw[5:6] * ur
    conv = mid + left * ml + right * mr + cb_ref[...]
    a = conv[:, :FF_TILE]
    act = a * _sigmoid(a) * conv[:, FF_TILE:]
    acc_sc[...] += _dot(act.astype(BF), wdn_ref[...])

    @pl.when(j == N_FF_TILES - 1)
    def _():
        y = x_ref[...] + m_ref[5:6, :] * acc_sc[...]
        if final:
            y = _rms(y, fg_ref[...])
        o_ref[...] = y


def _edge_masks(cols):
    c = np.arange(TOK) % cols
    ml = np.broadcast_to((c >= 1).astype(np.float32)[:, None], (TOK, LANES))
    mr = np.broadcast_to((c <= cols - 2).astype(np.float32)[:, None], (TOK, LANES))
    return jnp.asarray(ml), jnp.asarray(mr)


def _ffn(x, mods, layer, row0, row_step, g2, wup, cw, cb, wdn, masks, two_d, final_g=None):
    nblk = x.shape[0] // TOK
    in_specs = [pl.BlockSpec((TOK, D), lambda i, j: (i, 0)),
                _mod_spec(layer, row0, row_step),
                pl.BlockSpec((1, D), lambda i, j: (0, 0)),
                pl.BlockSpec((None, D, 2 * FF_TILE), lambda i, j: (j, 0, 0)),
                pl.BlockSpec((None, 9, 2 * FF_TILE), lambda i, j: (j, 0, 0)),
                pl.BlockSpec((None, 1, 2 * FF_TILE), lambda i, j: (j, 0, 0)),
                pl.BlockSpec((None, FF_TILE, D), lambda i, j: (j, 0, 0)),
                pl.BlockSpec((TOK, LANES), lambda i, j: (0, 0)),
                pl.BlockSpec((TOK, LANES), lambda i, j: (0, 0))]
    args = [x, mods, g2, wup, cw, cb, wdn, masks[0], masks[1]]
    if final_g is not None:
        in_specs.append(pl.BlockSpec((1, D), lambda i, j: (0, 0)))
        args.append(final_g)
    return pl.pallas_call(
        functools.partial(_ffn_kernel, two_d=two_d, final=final_g is not None),
        grid=(nblk, N_FF_TILES),
        in_specs=in_specs,
        out_specs=pl.BlockSpec((TOK, D), lambda i, j: (i, 0)),
        out_shape=jax.ShapeDtypeStruct(x.shape, F32),
        scratch_shapes=[pltpu.VMEM((TOK, D), BF), pltpu.VMEM((TOK, D), F32)],
        compiler_params=_cp(("parallel", "arbitrary")),
        name="conv_ffn",
    )(*args)


def kernel(x_prompt, x_sample, state_hgrn, cache_k, cache_v, c, c_ctx, ada_w, ada_b, norm1, norm2, final_norm,
           pool_w, pool_scale, hgrn_w_in, hgrn_lb, hgrn_norm, hgrn_w_out, diff_w_in, diff_lambda, diff_subln,
           diff_w_out, ffn_w_up, ffn_conv_w, ffn_conv_b, ffn_w_down):
    nbp, seq, _ = x_prompt.shape
    nbs, dseq, _ = x_sample.shape
    assert seq * (TOK // seq) == TOK and dseq == TOK and D_FF == ffn_w_down.shape[1]
    rows = dseq // GRID_W

    cond = jnp.concatenate([c_ctx[None, :], c, jnp.zeros((8 - 1 - nbs, D), F32)], axis=0)
    mods = _modulation(cond.T, ada_w, ada_b).reshape(DEPTH, 8, 6, D)

    groups = [
        dict(x=x_prompt.reshape(nbp * seq, D), row0=0, step=0, n_seq=seq, two_d=False,
             pool=_pool_tables(1, seq, TOK // seq), edge=_edge_masks(seq)),
        dict(x=x_sample.reshape(nbs * dseq, D), row0=1, step=1, n_seq=dseq, two_d=True,
             pool=_pool_tables(rows, GRID_W, 1), edge=_edge_masks(GRID_W)),
    ]
    hgrn_consts = _hgrn_masks()
    rope_tabs = _rope_tables()

    lbs = jax.nn.softmax(hgrn_lb.astype(F32), axis=1)
    lbs = jnp.cumsum(lbs, axis=1) - lbs[:, :1]

    new_state = new_k = new_v = None
    for i in range(DEPTH):
        kind, j = i % 3, i // 3
        g1 = norm1[i][None, :]
        g2 = norm2[i][None, :]
        wup = (ffn_w_up[i].reshape(D, 2, N_FF_TILES, FF_TILE).transpose(2, 0, 1, 3)
               .reshape(N_FF_TILES, D, 2 * FF_TILE).astype(BF))
        cw = (ffn_conv_w[i].reshape(9, 2, N_FF_TILES, FF_TILE).transpose(2, 0, 1, 3)
              .reshape(N_FF_TILES, 9, 2 * FF_TILE))
        cb = (ffn_conv_b[i].reshape(2, N_FF_TILES, FF_TILE).transpose(1, 0, 2)
              .reshape(N_FF_TILES, 1, 2 * FF_TILE))
        wdn = ffn_w_down[i].reshape(N_FF_TILES, FF_TILE, D).astype(BF)
        for gi, gr in enumerate(groups):
            x, row0, step = gr["x"], gr["row0"], gr["step"]
            is_prompt = gi == 0
            if kind == 0:
                x = _pool_layer(x, mods, i, row0, step, g1, gr["pool"][0], gr["pool"][1],
                                pool_w[j].astype(BF), pool_scale[j][None, :])
            elif kind == 1:
                proj = _inproj(x, mods, i, row0, step, g1, hgrn_w_in[j].astype(BF))
                lb = lbs[:, i].reshape(2, HEADS, HD).transpose(1, 0, 2)
                ng = hgrn_norm[j][None, :]
                if is_prompt:
                    og, new_state = _hgrn_scan(proj, gr["n_seq"], lb, ng, hgrn_consts, None, True)
                else:
                    (og,) = _hgrn_scan(proj, gr["n_seq"], lb, ng, hgrn_consts, state_hgrn[:, j], False)
                x = _outproj(og, hgrn_w_out[j].astype(BF), x, mods, i, row0, step)
            else:
                lam_init = 0.8 - 0.6 * math.exp(-0.3 * i)
                if is_prompt:
                    qkv = _inproj(x, mods, i, row0, step, g1, diff_w_in[j].astype(BF))
                    new_k, new_v = qkv[1], qkv[2]
                    cache = None
                else:
                    qkv = _inproj(x, mods, i, row0, step, g1, diff_w_in[j].astype(BF), rope_tabs)
                    cache = (cache_k[:, j].reshape(nbs, -1, D), cache_v[:, j].reshape(nbs, -1, D))
                og = _attention(qkv, gr["n_seq"], diff_lambda[j], diff_subln[j][None, :], lam_init, cache)
                x = _outproj(og, diff_w_out[j].astype(BF), x, mods, i, row0, step)
            fg = final_norm[None, :] if i == DEPTH - 1 else None
            gr["x"] = _ffn(x, mods, i, row0, step, g2, wup, cw, cb, wdn, gr["edge"], gr["two_d"], fg)

    y_prompt = groups[0]["x"].reshape(nbp, seq, D)
    y_sample = groups[1]["x"].reshape(nbs, dseq, D)
    new_state_hgrn = new_state[:, None]
    new_cache_k = new_k.reshape(nbp, 1, seq, HEADS, 2, DIFF_DH)
    new_cache_v = new_v.reshape(nbp, 1, seq, HEADS, 2 * DIFF_DH)
    return (y_prompt, y_sample, new_state_hgrn, new_cache_k, new_cache_v)
```

```python
import functools
import math

import numpy as np
import jax
import jax.numpy as jnp
from jax import lax
from jax.experimental import pallas as pl
from jax.experimental.pallas import tpu as pltpu

F32 = jnp.float32
BF = jnp.bfloat16

D = 1024
DEPTH = 4
GRID_W = 64
POOL_WINDOWS = (2, 4, 8, 16)
POOL_GROUP = D // len(POOL_WINDOWS)
HEADS = 8
HD = 128
DIFF_DH = 64
ROPE_BASE = 10000.0
D_FF = 2816
EPS = 1e-6

LANES = 128
SUBLANES = 8
TOK = 1024
FF_TILE = 256
FFN_ROWS_1D = 128
FFN_ROWS_2D = 256
N_FF_TILES = D_FF // FF_TILE
CHUNK = 128
ATT_QB = 256
MOD_TN = 1536
MOD_LC = 768
VMEM_LIMIT = 56 * 1024 * 1024


def _cp(sem, vmem=VMEM_LIMIT):
    return pltpu.CompilerParams(dimension_semantics=sem, vmem_limit_bytes=vmem)


def _sigmoid(x):
    return 1.0 / (1.0 + jnp.exp(-x))


def _rms(x, g):
    return x * lax.rsqrt(jnp.mean(x * x, axis=-1, keepdims=True) + EPS) * g


def _dot(a, b):
    return jnp.dot(a, b, preferred_element_type=F32)


def _dot_nt(a, b):
    return lax.dot_general(a, b, (((1,), (1,)), ((), ())), preferred_element_type=F32)


def _dot_tn(a, b):
    return lax.dot_general(a, b, (((0,), (0,)), ((), ())), preferred_element_type=F32)


def _mod_kernel(ct_ref, w_ref, b_ref, o_ref, sb_ref):
    @pl.when((pl.program_id(0) == 0) & (pl.program_id(1) == 0))
    def _():
        ct = ct_ref[...]
        s = ct * _sigmoid(ct)
        for r in range(3):
            sb_ref[r] = jnp.broadcast_to(s[:, r:r + 1], (D, LANES))

    for lc in range(MOD_TN // MOD_LC):
        c0 = lc * MOD_LC

        def body(i, accs, c0=c0):
            k0 = pl.multiple_of(i * SUBLANES, SUBLANES)
            wv = w_ref[pl.ds(k0, SUBLANES), c0:c0 + MOD_LC]
            out = []
            for r in range(3):
                sv = sb_ref[r, pl.ds(k0, SUBLANES), :]
                out.append(accs[r] + wv * jnp.tile(sv, (1, MOD_LC // LANES)))
            return tuple(out)

        accs = lax.fori_loop(0, D // SUBLANES, body,
                             tuple(jnp.zeros((SUBLANES, MOD_LC), F32) for _ in range(3)), unroll=4)
        for r in range(3):
            o_ref[r:r + 1, c0:c0 + MOD_LC] = (jnp.sum(accs[r], axis=0, keepdims=True)
                                              + b_ref[:, c0:c0 + MOD_LC])
        o_ref[3:8, c0:c0 + MOD_LC] = jnp.zeros((5, MOD_LC), F32)


def _modulation(cond_t, ada_w, ada_b):
    return pl.pallas_call(
        _mod_kernel,
        grid=(DEPTH, 6 * D // MOD_TN),
        in_specs=[pl.BlockSpec((D, 8), lambda l, j: (0, 0)),
                  pl.BlockSpec((None, D, MOD_TN), lambda l, j: (l, 0, j)),
                  pl.BlockSpec((None, 1, MOD_TN), lambda l, j: (l, 0, j))],
        out_specs=pl.BlockSpec((None, 8, MOD_TN), lambda l, j: (l, 0, j)),
        out_shape=jax.ShapeDtypeStruct((DEPTH, 8, 6 * D), F32),
        scratch_shapes=[pltpu.VMEM((3, D, LANES), F32)],
        compiler_params=_cp(("arbitrary", "arbitrary")),
        name="modulation",
    )(cond_t, ada_w, ada_b.reshape(DEPTH, 1, 6 * D))


def _mod_spec(layer, row0, row_step):
    return pl.BlockSpec((None, None, 6, D), lambda i, *_: (layer, row0 + row_step * i, 0, 0))


def _pool_kernel(x_ref, m_ref, g_ref, p_ref, ic_ref, w_ref, sc_ref, o_ref):
    x = x_ref[...]
    m = m_ref[...]
    h = _rms(x, g_ref[...]) * (1.0 + m[1:2]) + m[0:1]
    for g in range(len(POOL_WINDOWS)):
        sl = slice(g * POOL_GROUP, (g + 1) * POOL_GROUP)
        hg = h[:, sl]
        h1 = hg.astype(BF)
        r1 = hg - h1.astype(F32)
        h2 = r1.astype(BF)
        h3 = (r1 - h2.astype(F32)).astype(BF)
        p = p_ref[g]
        s = _dot(p, h1) + _dot(p, h2) + _dot(p, h3)
        ic = ic_ref[g]
        d = s * jnp.concatenate([ic, ic], axis=1) - hg
        y = _dot(d.astype(BF), w_ref[g].astype(BF)) * sc_ref[:, sl]
        o_ref[:, sl] = x[:, sl] + m[2:3, sl] * y


def _pool_tables(rows, cols, segs):
    n = rows * cols
    r = np.arange(n) // cols
    c = np.arange(n) % cols
    mats, invs = [], []
    for w in POOL_WINDOWS:
        def memb(pos, size):
            lo = np.clip(pos - w // 2, 0, size)
            hi = np.clip(pos + w // 2, 0, size)
            j = np.arange(size)
            return ((j[None, :] >= lo[:, None]) & (j[None, :] < hi[:, None])), (hi - lo)
        mr, cr = memb(np.arange(rows), rows)
        mc, cc = memb(np.arange(cols), cols)
        p = mr[r][:, r] & mc[c][:, c]
        cnt = (cr[r] * cc[c]).astype(np.float64)
        mats.append(np.kron(np.eye(segs), p.astype(np.float64)))
        invs.append(np.tile(1.0 / cnt, segs))
    p01 = jnp.asarray(np.stack(mats), dtype=BF)
    inv = jnp.asarray(np.broadcast_to(np.stack(invs)[:, :, None], (len(POOL_WINDOWS), n * segs, LANES)),
                      dtype=F32)
    return p01, inv


def _pool_layer(x, mods, layer, row0, row_step, g1, p01, inv, w, scale):
    nblk = x.shape[0] // TOK
    const3 = lambda i: (0, 0, 0)
    return pl.pallas_call(
        _pool_kernel,
        grid=(nblk,),
        in_specs=[pl.BlockSpec((TOK, D), lambda i: (i, 0)),
                  _mod_spec(layer, row0, row_step),
                  pl.BlockSpec((1, D), lambda i: (0, 0)),
                  pl.BlockSpec((4, TOK, TOK), const3),
                  pl.BlockSpec((4, TOK, LANES), const3),
                  pl.BlockSpec((4, POOL_GROUP, POOL_GROUP), const3),
                  pl.BlockSpec((1, D), lambda i: (0, 0))],
        out_specs=pl.BlockSpec((TOK, D), lambda i: (i, 0)),
        out_shape=jax.ShapeDtypeStruct(x.shape, F32),
        compiler_params=_cp(("parallel",)),
        name="pool_mixer",
    )(x, mods, g1, p01, inv, w, scale)


def _rope_tile(y, cos, sa, sb):
    outs = []
    for c in range(D // LANES):
        yc = y[:, c * LANES:(c + 1) * LANES]
        up = pltpu.roll(yc, LANES - 16, axis=1)
        dn = pltpu.roll(yc, 16, axis=1)
        outs.append(yc * cos + up * sa + dn * sb)
    return jnp.concatenate(outs, axis=1)


def _inproj_kernel(x_ref, m_ref, g_ref, w_ref, *rest, rope):
    if rope:
        cos_ref, sa_ref, sb_ref, o_ref, h_sc = rest
    else:
        o_ref, h_sc = rest
    j = pl.program_id(1)

    @pl.when(j == 0)
    def _():
        m = m_ref[...]
        h = _rms(x_ref[...], g_ref[...]) * (1.0 + m[1:2]) + m[0:1]
        h_sc[...] = h.astype(BF)

    y = _dot(h_sc[...], w_ref[...].astype(BF))
    if rope:
        @pl.when(j < 2)
        def _():
            o_ref[...] = _rope_tile(y, cos_ref[...], sa_ref[...], sb_ref[...])

        @pl.when(j >= 2)
        def _():
            o_ref[...] = y
    else:
        o_ref[...] = y


def _inproj(x, mods, layer, row0, row_step, g1, w, rope_tabs=None):
    t = x.shape[0]
    nblk = t // TOK
    ntile = w.shape[1] // D
    in_specs = [pl.BlockSpec((TOK, D), lambda i, j: (i, 0)),
                _mod_spec(layer, row0, row_step),
                pl.BlockSpec((1, D), lambda i, j: (0, 0)),
                pl.BlockSpec((D, D), lambda i, j: (0, j))]
    args = [x, mods, g1, w]
    if rope_tabs is not None:
        in_specs += [pl.BlockSpec((TOK, LANES), lambda i, j: (0, 0))] * 3
        args += list(rope_tabs)
    return pl.pallas_call(
        functools.partial(_inproj_kernel, rope=rope_tabs is not None),
        grid=(nblk, ntile),
        in_specs=in_specs,
        out_specs=pl.BlockSpec((None, TOK, D), lambda i, j: (j, i, 0)),
        out_shape=jax.ShapeDtypeStruct((ntile, t, D), F32),
        scratch_shapes=[pltpu.VMEM((TOK, D), BF)],
        compiler_params=_cp(("parallel", "arbitrary")),
        name="in_proj",
    )(*args)


def _rope_tables():
    t = np.arange(TOK)
    half = DIFF_DH // 2
    nf = half // 2
    f32 = np.float32
    inv = np.power(f32(ROPE_BASE), -np.arange(nf, dtype=f32) / f32(nf)).astype(f32)
    lane = np.arange(LANES) % DIFF_DH
    use_col = lane >= half
    fidx = (lane % half) % nf
    second = (lane % half) >= nf
    pos = np.where(use_col[None, :], (t % GRID_W).astype(f32)[:, None], (t // GRID_W).astype(f32)[:, None])
    ang = (pos * inv[fidx][None, :]).astype(f32)
    cos, sin = np.cos(ang).astype(f32), np.sin(ang).astype(f32)
    sa = np.where(second[None, :], f32(0), -sin)
    sb = np.where(second[None, :], sin, f32(0))
    return jnp.asarray(cos), jnp.asarray(sa), jnp.asarray(sb)


def _outproj_kernel(a_ref, w_ref, x_ref, m_ref, o_ref):
    y = _dot(a_ref[...].astype(BF), w_ref[...].astype(BF))
    o_ref[...] = x_ref[...] + m_ref[2:3, :] * y


def _outproj(a, w, x, mods, layer, row0, row_step):
    nblk = x.shape[0] // TOK
    return pl.pallas_call(
        _outproj_kernel,
        grid=(nblk,),
        in_specs=[pl.BlockSpec((TOK, D), lambda i: (i, 0)),
                  pl.BlockSpec((D, D), lambda i: (0, 0)),
                  pl.BlockSpec((TOK, D), lambda i: (i, 0)),
                  _mod_spec(layer, row0, row_step)],
        out_specs=pl.BlockSpec((TOK, D), lambda i: (i, 0)),
        out_shape=jax.ShapeDtypeStruct(x.shape, F32),
        compiler_params=_cp(("parallel",)),
        name="out_proj",
    )(a, w, x, mods)


HGRN_LEVELS = (1, 2, 4, 8, 16, 32, 64)


def _hgrn_masks():
    t = np.arange(CHUNK)[:, None]
    s = np.arange(CHUNK)[None, :]
    fwd = [(t == s)]
    for b in HGRN_LEVELS:
        fwd.append((t // (2 * b) == s // (2 * b)) & (t % (2 * b) >= b) & (s % (2 * b) < b))
    fwd = np.stack(fwd).astype(np.float32)
    bwd = np.transpose(fwd, (0, 2, 1))
    tri = np.stack([(s <= t), (s >= t)]).astype(np.float32)
    return jnp.asarray(fwd), jnp.asarray(bwd), jnp.asarray(tri)


def _hgrn_chunk(q, v, z, lb, tri, masks, st, rev):
    f = lb + (1.0 - lb) * _sigmoid(z)
    lf = jnp.log(f)
    k = 1.0 - f
    cum = jnp.dot(tri, lf, preferred_element_type=F32, precision=lax.Precision.HIGHEST)
    tot = cum[0:1] if rev else cum[CHUNK - 1:CHUNK]
    qb = q.astype(BF)

    s = masks[0] * _dot_nt(qb, k.astype(BF))

    pos = lax.broadcasted_iota(jnp.int32, (CHUNK, HD), 0)
    odd = (pos & 1) == 1
    e = jnp.where(odd, 1.0, f) if rev else jnp.where(odd, f, 1.0)
    s = s + masks[1] * _dot_nt((q * e).astype(BF), (k * e).astype(BF))

    lfn = pltpu.roll(lf, CHUNK - 1, axis=0)
    lfp = pltpu.roll(lf, 1, axis=0)
    p4 = pos & 3
    if rev:
        ex = jnp.where(p4 == 0, lf + lfn, jnp.where(p4 == 1, lf, jnp.where(p4 == 2, 0.0, lfp)))
    else:
        ex = jnp.where(p4 == 0, lfn, jnp.where(p4 == 1, 0.0, jnp.where(p4 == 2, lf, lf + lfp)))
    e = jnp.exp(ex)
    s = s + masks[2] * _dot_nt((q * e).astype(BF), (k * e).astype(BF))

    for li, b in enumerate(HGRN_LEVELS[2:]):
        c3 = cum.reshape(CHUNK // (2 * b), 2 * b, HD)
        ridx = b if rev else b - 1
        e = jnp.exp(-jnp.abs(c3 - c3[:, ridx:ridx + 1, :])).reshape(CHUNK, HD)
        s = s + masks[3 + li] * _dot_nt((q * e).astype(BF), (k * e).astype(BF))

    vb = v.astype(BF)
    qh = (q * jnp.exp(cum)).astype(BF)
    kh = (k * jnp.exp(tot - cum)).astype(BF)
    o = _dot(s.astype(BF), vb) + _dot_nt(qh, st.astype(BF))
    st_new = st * jnp.exp(tot) + _dot_tn(vb, kh)
    return o, st_new


def _hgrn_kernel(*refs, n_chunks, has_s0, emit_state):
    q_ref, v_ref, g_ref, zf_ref, zb_ref, lb_ref, ng_ref, mf_ref, mb_ref, tri_ref = refs[:10]
    rest = refs[10:]
    if has_s0:
        s0_ref, rest = rest[0], rest[1:]
    o_ref, rest = rest[0], rest[1:]
    if emit_state:
        st_ref, rest = rest[0], rest[1:]
    of_sc, ob_sc = rest

    lbf = lb_ref[0:1, :]
    lbb = lb_ref[1:2, :]
    if has_s0:
        st0 = (s0_ref[0].T, s0_ref[1].T)
    else:
        st0 = (jnp.zeros((HD, HD), F32), jnp.zeros((HD, HD), F32))

    def body(i, carry):
        stf, stb = carry
        cf = pl.multiple_of(i * CHUNK, CHUNK)
        cb = pl.multiple_of((n_chunks - 1 - i) * CHUNK, CHUNK)
        o_f, stf = _hgrn_chunk(q_ref[pl.ds(cf, CHUNK), :], v_ref[pl.ds(cf, CHUNK), :],
                               zf_ref[pl.ds(cf, CHUNK), :], lbf, tri_ref[0], mf_ref, stf, False)
        o_b, stb = _hgrn_chunk(q_ref[pl.ds(cb, CHUNK), :], v_ref[pl.ds(cb, CHUNK), :],
                               zb_ref[pl.ds(cb, CHUNK), :], lbb, tri_ref[1], mb_ref, stb, True)
        of_sc[pl.ds(cf, CHUNK), :] = o_f
        ob_sc[pl.ds(cb, CHUNK), :] = o_b
        return stf, stb

    stf, stb = lax.fori_loop(0, n_chunks, body, st0)
    if emit_state:
        st_ref[0] = stf.T
        st_ref[1] = stb.T
    o = of_sc[...] + ob_sc[...]
    g = g_ref[...]
    o_ref[...] = _rms(o, ng_ref[...]) * (g * _sigmoid(g))


def _hgrn_scan(proj, n_seq, lb, ng, consts, s0=None, emit_state=False):
    t = proj.shape[1]
    nb = t // n_seq
    mf, mb, tri = consts

    def pspec(k):
        return pl.BlockSpec((None, n_seq, HD), lambda b, h, k=k: (k, b, h))

    c3 = lambda b, h: (0, 0, 0)
    in_specs = [pspec(0), pspec(1), pspec(2), pspec(3), pspec(4),
                pl.BlockSpec((None, 2, HD), lambda b, h: (h, 0, 0)),
                pl.BlockSpec((1, HD), lambda b, h: (0, 0)),
                pl.BlockSpec(mf.shape, c3), pl.BlockSpec(mb.shape, c3), pl.BlockSpec(tri.shape, c3)]
    args = [proj, proj, proj, proj, proj, lb, ng, mf, mb, tri]
    if s0 is not None:
        in_specs.append(pl.BlockSpec((None, 2, None, HD, HD), lambda b, h: (b, 0, h, 0, 0)))
        args.append(s0)
    out_specs = [pl.BlockSpec((n_seq, HD), lambda b, h: (b, h))]
    out_shape = [jax.ShapeDtypeStruct((t, D), F32)]
    if emit_state:
        out_specs.append(pl.BlockSpec((None, 2, None, HD, HD), lambda b, h: (b, 0, h, 0, 0)))
        out_shape.append(jax.ShapeDtypeStruct((nb, 2, HEADS, HD, HD), F32))
    res = pl.pallas_call(
        functools.partial(_hgrn_kernel, n_chunks=n_seq // CHUNK, has_s0=s0 is not None, emit_state=emit_state),
        grid=(nb, HEADS),
        in_specs=in_specs,
        out_specs=out_specs,
        out_shape=out_shape,
        scratch_shapes=[pltpu.VMEM((n_seq, HD), F32), pltpu.VMEM((n_seq, HD), F32)],
        compiler_params=_cp(("parallel", "parallel")),
        name="hgrn_scan",
    )(*args)
    return res


def _attn_kernel(*refs, lam_init, has_cache):
    lam_ref, q_ref = refs[:2]
    if has_cache:
        kc_ref, vc_ref, kn_ref, vn_ref, sg_ref, o_ref = refs[2:]
        k_refs, v_refs = (kc_ref, kn_ref), (vc_ref, vn_ref)
    else:
        kn_ref, vn_ref, sg_ref, o_ref = refs[2:]
        k_refs, v_refs = (kn_ref,), (vn_ref,)

    lp = lam_ref[...]
    lam = (jnp.exp(jnp.sum(lp[0:1] * lp[1:2], axis=-1, keepdims=True))
           - jnp.exp(jnp.sum(lp[2:3] * lp[3:4], axis=-1, keepdims=True)) + lam_init)

    q = q_ref[...] * (DIFF_DH ** -0.5)
    first = lax.broadcasted_iota(jnp.int32, (1, HD), 1) < DIFF_DH
    qs = (jnp.where(first, q, 0.0).astype(BF), jnp.where(first, 0.0, q).astype(BF))
    ks = [r[...].astype(BF) for r in k_refs]
    vs = [r[...].astype(BF) for r in v_refs]

    outs = []
    for qm in qs:
        sc = [_dot_nt(qm, kk) for kk in ks]
        mx = functools.reduce(jnp.maximum, [jnp.max(s, axis=-1, keepdims=True) for s in sc])
        es = [jnp.exp(s - mx) for s in sc]
        den = functools.reduce(jnp.add, [jnp.sum(e, axis=-1, keepdims=True) for e in es])
        pv = functools.reduce(jnp.add, [_dot(e.astype(BF), vv) for e, vv in zip(es, vs)])
        outs.append(pv * (1.0 / den))
    o = outs[0] - lam * outs[1]
    o_ref[...] = _rms(o, sg_ref[...]) * (1.0 - lam_init)


def _attention(qkv, n_seq, lam_p, subln, lam_init, cache=None):
    t = qkv.shape[1]
    nb = t // n_seq
    nqb = n_seq // ATT_QB
    in_specs = [pl.BlockSpec((4, DIFF_DH), lambda b, h, i: (0, 0)),
                pl.BlockSpec((None, ATT_QB, HD), lambda b, h, i: (0, b * nqb + i, h))]
    args = [lam_p, qkv]
    if cache is not None:
        past = cache[0].shape[1]
        in_specs += [pl.BlockSpec((None, past, HD), lambda b, h, i: (b, 0, h))] * 2
        args += [cache[0], cache[1]]
    in_specs += [pl.BlockSpec((None, n_seq, HD), lambda b, h, i: (1, b, h)),
                 pl.BlockSpec((None, n_seq, HD), lambda b, h, i: (2, b, h)),
                 pl.BlockSpec((1, HD), lambda b, h, i: (0, 0))]
    args += [qkv, qkv, subln]
    return pl.pallas_call(
        functools.partial(_attn_kernel, lam_init=lam_init, has_cache=cache is not None),
        grid=(nb, HEADS, nqb),
        in_specs=in_specs,
        out_specs=pl.BlockSpec((ATT_QB, HD), lambda b, h, i: (b * nqb + i, h)),
        out_shape=jax.ShapeDtypeStruct((t, D), F32),
        compiler_params=_cp(("parallel", "parallel", "arbitrary")),
        name="diff_attention",
    )(*args)


def _shift_tokens(u, d, first, cols):
    n = u.shape[0]
    r = pltpu.roll(u, (-d) % n, axis=0)
    row = lax.broadcasted_iota(jnp.int32, (SUBLANES, u.shape[1]), 0)
    edge = row == (0 if d < 0 else SUBLANES - 1)
    want = 0 if d < 0 else cols - 1
    pieces, prev = [], 0
    for e in range(n):
        if (first + e) % cols == want:
            g = e - e % SUBLANES
            if g > prev:
                pieces.append(r[prev:g])
            pieces.append(jnp.where(edge, 0.0, r[g:g + SUBLANES]))
            prev = g + SUBLANES
    if prev < n:
        pieces.append(r[prev:])
    return jnp.concatenate(pieces, axis=0) if len(pieces) > 1 else pieces[0]


def _dwconv_rows(u_ref, r0, n, cw, cb, cols, two_d):
    halo = cols if two_d else SUBLANES
    lo, hi = r0 - halo, r0 + n + halo
    zero = jnp.zeros((halo, u_ref.shape[1]), F32)
    ext = jnp.concatenate(([zero] if lo < 0 else []) + [u_ref[max(lo, 0):min(hi, TOK), :]]
                          + ([zero] if hi > TOK else []), axis=0)
    ul = _shift_tokens(ext, -1, lo, cols)
    ur = _shift_tokens(ext, 1, lo, cols)

    def taps(k, a):
        return cw[k:k + 1] * ul[a:a + n] + cw[k + 1:k + 2] * ext[a:a + n] + cw[k + 2:k + 3] * ur[a:a + n]

    out = taps(3, halo) + cb
    if two_d:
        out = out + taps(0, 0) + taps(6, 2 * halo)
    return out


def _ffn_kernel(*refs, cols, two_d, final):
    x_ref, m_ref, g_ref, wa_ref, wv_ref, cwa_ref, cwv_ref, cba_ref, cbv_ref, wdn_ref = refs[:10]
    rest = refs[10:]
    if final:
        fg_ref, rest = rest[0], rest[1:]
    o_ref, h_sc, acc_sc, ua0_sc, uv0_sc, ua1_sc, uv1_sc = rest
    j = pl.program_id(1)

    @pl.when(j == 0)
    def _():
        m = m_ref[...]
        h = _rms(x_ref[...], g_ref[...]) * (1.0 + m[4:5]) + m[3:4]
        h_sc[...] = h.astype(BF)
        acc_sc[...] = jnp.zeros_like(acc_sc)
        ua1_sc[...] = jnp.zeros_like(ua1_sc)
        uv1_sc[...] = jnp.zeros_like(uv1_sc)

    def step(ua_w, uv_w, ua_r, uv_r):
        wa = wa_ref[...].astype(BF)
        wv = wv_ref[...].astype(BF)
        wd = wdn_ref[...].astype(BF)
        cwa, cwv, cba, cbv = cwa_ref[...], cwv_ref[...], cba_ref[...], cbv_ref[...]
        keep = jnp.where(j > 0, 1.0, 0.0).astype(F32)
        nrow = FFN_ROWS_2D if two_d else FFN_ROWS_1D
        for r0 in range(0, TOK, nrow):
            rows = slice(r0, r0 + nrow)
            h = h_sc[rows, :]
            ua_w[rows, :] = _dot(h, wa)
            uv_w[rows, :] = _dot(h, wv)
            a = _dwconv_rows(ua_r, r0, nrow, cwa, cba, cols, two_d)
            v = _dwconv_rows(uv_r, r0, nrow, cwv, cbv, cols, two_d)
            act = a * _sigmoid(a) * (v * keep)
            acc_sc[rows, :] += _dot(act.astype(BF), wd)

    @pl.when(j % 2 == 0)
    def _():
        step(ua0_sc, uv0_sc, ua1_sc, uv1_sc)

    @pl.when(j % 2 == 1)
    def _():
        step(ua1_sc, uv1_sc, ua0_sc, uv0_sc)

    @pl.when(j == N_FF_TILES)
    def _():
        y = x_ref[...] + m_ref[5:6, :] * acc_sc[...]
        if final:
            y = _rms(y, fg_ref[...])
        o_ref[...] = y


def _ffn(x, mods, layer, row0, row_step, g2, wup, cw, cb, wdn, cols, two_d, final_g=None):
    nblk = x.shape[0] // TOK
    nt = N_FF_TILES
    up = lambda j: jnp.minimum(j, nt - 1)
    dn = lambda j: jnp.maximum(j - 1, 0)
    in_specs = [pl.BlockSpec((TOK, D), lambda i, j: (i, 0)),
                _mod_spec(layer, row0, row_step),
                pl.BlockSpec((1, D), lambda i, j: (0, 0)),
                pl.BlockSpec((None, D, FF_TILE), lambda i, j: (layer, 0, up(j))),
                pl.BlockSpec((None, D, FF_TILE), lambda i, j: (layer, 0, nt + up(j))),
                pl.BlockSpec((None, 9, FF_TILE), lambda i, j: (layer, 0, dn(j))),
                pl.BlockSpec((None, 9, FF_TILE), lambda i, j: (layer, 0, nt + dn(j))),
                pl.BlockSpec((None, 1, FF_TILE), lambda i, j: (layer, 0, dn(j))),
                pl.BlockSpec((None, 1, FF_TILE), lambda i, j: (layer, 0, nt + dn(j))),
                pl.BlockSpec((None, FF_TILE, D), lambda i, j: (layer, dn(j), 0))]
    args = [x, mods, g2, wup, wup, cw, cw, cb, cb, wdn]
    if final_g is not None:
        in_specs.append(pl.BlockSpec((1, D), lambda i, j: (0, 0)))
        args.append(final_g)
    return pl.pallas_call(
        functools.partial(_ffn_kernel, cols=cols, two_d=two_d, final=final_g is not None),
        grid=(nblk, N_FF_TILES + 1),
        in_specs=in_specs,
        out_specs=pl.BlockSpec((TOK, D), lambda i, j: (i, 0)),
        out_shape=jax.ShapeDtypeStruct(x.shape, F32),
        scratch_shapes=[pltpu.VMEM((TOK, D), BF), pltpu.VMEM((TOK, D), F32),
                        *[pltpu.VMEM((TOK, FF_TILE), F32) for _ in range(4)]],
        compiler_params=_cp(("parallel", "arbitrary")),
        name="conv_ffn",
    )(*args)


def kernel(x_prompt, x_sample, state_hgrn, cache_k, cache_v, c, c_ctx, ada_w, ada_b, norm1, norm2, final_norm,
           pool_w, pool_scale, hgrn_w_in, hgrn_lb, hgrn_norm, hgrn_w_out, diff_w_in, diff_lambda, diff_subln,
           diff_w_out, ffn_w_up, ffn_conv_w, ffn_conv_b, ffn_w_down):
    nbp, seq, _ = x_prompt.shape
    nbs, dseq, _ = x_sample.shape
    assert seq * (TOK // seq) == TOK and dseq == TOK and D_FF == ffn_w_down.shape[1]
    rows = dseq // GRID_W

    cond = jnp.concatenate([c_ctx[None, :], c, jnp.zeros((8 - 1 - nbs, D), F32)], axis=0)
    mods = _modulation(cond.T, ada_w, ada_b).reshape(DEPTH, 8, 6, D)

    groups = [
        dict(x=x_prompt.reshape(nbp * seq, D), row0=0, step=0, n_seq=seq, two_d=False, cols=seq,
             pool=_pool_tables(1, seq, TOK // seq)),
        dict(x=x_sample.reshape(nbs * dseq, D), row0=1, step=1, n_seq=dseq, two_d=True, cols=GRID_W,
             pool=_pool_tables(rows, GRID_W, 1)),
    ]
    conv_w = ffn_conv_w.reshape(DEPTH, 9, 2 * D_FF)
    conv_b = ffn_conv_b.reshape(DEPTH, 1, 2 * D_FF)
    hgrn_consts = _hgrn_masks()
    rope_tabs = _rope_tables()

    lbs = jax.nn.softmax(hgrn_lb.astype(F32), axis=1)
    lbs = jnp.cumsum(lbs, axis=1) - lbs[:, :1]

    new_state = new_k = new_v = None
    for i in range(DEPTH):
        kind, j = i % 3, i // 3
        g1 = norm1[i][None, :]
        g2 = norm2[i][None, :]
        for gi, gr in enumerate(groups):
            x, row0, step = gr["x"], gr["row0"], gr["step"]
            is_prompt = gi == 0
            if kind == 0:
                x = _pool_layer(x, mods, i, row0, step, g1, gr["pool"][0], gr["pool"][1],
                                pool_w[j], pool_scale[j][None, :])
            elif kind == 1:
                proj = _inproj(x, mods, i, row0, step, g1, hgrn_w_in[j])
                lb = lbs[:, i].reshape(2, HEADS, HD).transpose(1, 0, 2)
                ng = hgrn_norm[j][None, :]
                if is_prompt:
                    og, new_state = _hgrn_scan(proj, gr["n_seq"], lb, ng, hgrn_consts, None, True)
                else:
                    (og,) = _hgrn_scan(proj, gr["n_seq"], lb, ng, hgrn_consts, state_hgrn[:, j], False)
                x = _outproj(og, hgrn_w_out[j], x, mods, i, row0, step)
            else:
                lam_init = 0.8 - 0.6 * math.exp(-0.3 * i)
                if is_prompt:
                    qkv = _inproj(x, mods, i, row0, step, g1, diff_w_in[j])
                    new_k, new_v = qkv[1], qkv[2]
                    cache = None
                else:
                    qkv = _inproj(x, mods, i, row0, step, g1, diff_w_in[j], rope_tabs)
                    cache = (cache_k[:, j].reshape(nbs, -1, D), cache_v[:, j].reshape(nbs, -1, D))
                og = _attention(qkv, gr["n_seq"], diff_lambda[j], diff_subln[j][None, :], lam_init, cache)
                x = _outproj(og, diff_w_out[j], x, mods, i, row0, step)
            fg = final_norm[None, :] if i == DEPTH - 1 else None
            gr["x"] = _ffn(x, mods, i, row0, step, g2, ffn_w_up, conv_w, conv_b, ffn_w_down,
                           gr["cols"], gr["two_d"], fg)

    y_prompt = groups[0]["x"].reshape(nbp, seq, D)
    y_sample = groups[1]["x"].reshape(nbs, dseq, D)
    new_state_hgrn = new_state[:, None]
    new_cache_k = new_k.reshape(nbp, 1, seq, HEADS, 2, DIFF_DH)
    new_cache_v = new_v.reshape(nbp, 1, seq, HEADS, 2 * DIFF_DH)
    return (y_prompt, y_sample, new_state_hgrn, new_cache_k, new_cache_v)
```

```python
import functools
import math

import numpy as np
import jax
import jax.numpy as jnp
from jax import lax
from jax.experimental import pallas as pl
from jax.experimental.pallas import tpu as pltpu

F32 = jnp.float32
BF = jnp.bfloat16

D = 1024
DEPTH = 4
GRID_W = 64
POOL_WINDOWS = (2, 4, 8, 16)
POOL_GROUP = D // len(POOL_WINDOWS)
HEADS = 8
HD = 128
DIFF_DH = 64
ROPE_BASE = 10000.0
D_FF = 2816
EPS = 1e-6

LANES = 128
SUBLANES = 8
TOK = 1024
FF_TILE = 256
FFN_ROWS_1D = 512
FFN_ROWS_2D = 512
N_FF_TILES = D_FF // FF_TILE
CHUNK = 128
HGRN_HEADS_PROMPT = 2
HGRN_HEADS_SAMPLE = 2
HGRN_UNROLL_SAMPLE = 1
ATT_QB = 256
ATT_HEADS_PROMPT = 8
ATT_HEADS_SAMPLE = 1
MOD_TN = 1536
MOD_LC = 768
VMEM_LIMIT = 56 * 1024 * 1024


def _cp(sem, vmem=VMEM_LIMIT):
    return pltpu.CompilerParams(dimension_semantics=sem, vmem_limit_bytes=vmem)


def _sigmoid(x):
    return 1.0 / (1.0 + jnp.exp(-x))


def _rms(x, g):
    return x * lax.rsqrt(jnp.mean(x * x, axis=-1, keepdims=True) + EPS) * g


def _dot(a, b):
    return jnp.dot(a, b, preferred_element_type=F32)


def _dot_nt(a, b):
    return lax.dot_general(a, b, (((1,), (1,)), ((), ())), preferred_element_type=F32)


def _dot_tn(a, b):
    return lax.dot_general(a, b, (((0,), (0,)), ((), ())), preferred_element_type=F32)


def _mod_kernel(ct_ref, w_ref, b_ref, o_ref, sb_ref):
    @pl.when((pl.program_id(0) == 0) & (pl.program_id(1) == 0))
    def _():
        ct = ct_ref[...]
        s = ct * _sigmoid(ct)
        for r in range(3):
            sb_ref[r] = jnp.broadcast_to(s[:, r:r + 1], (D, LANES))

    for lc in range(MOD_TN // MOD_LC):
        c0 = lc * MOD_LC

        def body(i, accs, c0=c0):
            k0 = pl.multiple_of(i * SUBLANES, SUBLANES)
            wv = w_ref[pl.ds(k0, SUBLANES), c0:c0 + MOD_LC]
            out = []
            for r in range(3):
                sv = sb_ref[r, pl.ds(k0, SUBLANES), :]
                out.append(accs[r] + wv * jnp.tile(sv, (1, MOD_LC // LANES)))
            return tuple(out)

        accs = lax.fori_loop(0, D // SUBLANES, body,
                             tuple(jnp.zeros((SUBLANES, MOD_LC), F32) for _ in range(3)), unroll=4)
        for r in range(3):
            o_ref[r:r + 1, c0:c0 + MOD_LC] = (jnp.sum(accs[r], axis=0, keepdims=True)
                                              + b_ref[:, c0:c0 + MOD_LC])
        o_ref[3:8, c0:c0 + MOD_LC] = jnp.zeros((5, MOD_LC), F32)


def _modulation(cond_t, ada_w, ada_b):
    return pl.pallas_call(
        _mod_kernel,
        grid=(DEPTH, 6 * D // MOD_TN),
        in_specs=[pl.BlockSpec((D, 8), lambda l, j: (0, 0)),
                  pl.BlockSpec((None, D, MOD_TN), lambda l, j: (l, 0, j)),
                  pl.BlockSpec((None, 1, MOD_TN), lambda l, j: (l, 0, j))],
        out_specs=pl.BlockSpec((None, 8, MOD_TN), lambda l, j: (l, 0, j)),
        out_shape=jax.ShapeDtypeStruct((DEPTH, 8, 6 * D), F32),
        scratch_shapes=[pltpu.VMEM((3, D, LANES), F32)],
        compiler_params=_cp(("arbitrary", "arbitrary")),
        name="modulation",
    )(cond_t, ada_w, ada_b.reshape(DEPTH, 1, 6 * D))


def _mod_spec(layer, row0, row_step):
    return pl.BlockSpec((None, None, 6, D), lambda i, *_: (layer, row0 + row_step * i, 0, 0))


def _shift_grid_rows(a, n):
    z = jnp.zeros((abs(n), a.shape[1]), a.dtype)
    if n > 0:
        return jnp.concatenate([a[n:], z], axis=0)
    return jnp.concatenate([z, a[:n]], axis=0)


def _box_sum(x, w, shift):
    fwd = bwd = x
    k = 1
    while k < w // 2:
        fwd = fwd + shift(fwd, k)
        bwd = bwd + shift(bwd, -k)
        k *= 2
    return shift(bwd, -1) + fwd


def _pool_kernel(x_ref, m_ref, g_ref, ic_ref, w_ref, sc_ref, o_ref, *, cols, two_d):
    x = x_ref[...]
    m = m_ref[...]
    h = _rms(x, g_ref[...]) * (1.0 + m[1:2]) + m[0:1]
    for g, w in enumerate(POOL_WINDOWS):
        sl = slice(g * POOL_GROUP, (g + 1) * POOL_GROUP)
        hg = h[:, sl]
        s = _box_sum(hg, w, lambda a, d: _shift_tokens(a, d, 0, cols))
        if two_d:
            s = _box_sum(s, w, lambda a, d: _shift_grid_rows(a, d * cols))
        ic = ic_ref[g]
        d = s * jnp.concatenate([ic, ic], axis=1) - hg
        y = _dot(d.astype(BF), w_ref[g].astype(BF)) * sc_ref[:, sl]
        o_ref[:, sl] = x[:, sl] + m[2:3, sl] * y


def _pool_counts(rows, cols, segs):
    n = rows * cols
    r = np.arange(n) // cols
    c = np.arange(n) % cols
    invs = []
    for w in POOL_WINDOWS:
        def size(pos, length):
            return np.clip(pos + w // 2, 0, length) - np.clip(pos - w // 2, 0, length)
        cnt = (size(np.arange(rows), rows)[r] * size(np.arange(cols), cols)[c]).astype(np.float64)
        invs.append(np.tile(1.0 / cnt, segs))
    return jnp.asarray(np.broadcast_to(np.stack(invs)[:, :, None], (len(POOL_WINDOWS), n * segs, LANES)),
                       dtype=F32)


def _pool_layer(x, mods, layer, row0, row_step, g1, inv, w, scale, cols, two_d):
    nblk = x.shape[0] // TOK
    const3 = lambda i: (0, 0, 0)
    return pl.pallas_call(
        functools.partial(_pool_kernel, cols=cols, two_d=two_d),
        grid=(nblk,),
        in_specs=[pl.BlockSpec((TOK, D), lambda i: (i, 0)),
                  _mod_spec(layer, row0, row_step),
                  pl.BlockSpec((1, D), lambda i: (0, 0)),
                  pl.BlockSpec((4, TOK, LANES), const3),
                  pl.BlockSpec((4, POOL_GROUP, POOL_GROUP), const3),
                  pl.BlockSpec((1, D), lambda i: (0, 0))],
        out_specs=pl.BlockSpec((TOK, D), lambda i: (i, 0)),
        out_shape=jax.ShapeDtypeStruct(x.shape, F32),
        compiler_params=_cp(("parallel",)),
        name="pool_mixer",
    )(x, mods, g1, inv, w, scale)


def _rope_tile(y, cos, sa, sb):
    outs = []
    for c in range(D // LANES):
        yc = y[:, c * LANES:(c + 1) * LANES]
        up = pltpu.roll(yc, LANES - 16, axis=1)
        dn = pltpu.roll(yc, 16, axis=1)
        outs.append(yc * cos + up * sa + dn * sb)
    return jnp.concatenate(outs, axis=1)


def _inproj_kernel(x_ref, m_ref, g_ref, w_ref, *rest, rope):
    if rope:
        cos_ref, sa_ref, sb_ref, o_ref, h_sc = rest
    else:
        o_ref, h_sc = rest
    j = pl.program_id(1)

    @pl.when(j == 0)
    def _():
        m = m_ref[...]
        h = _rms(x_ref[...], g_ref[...]) * (1.0 + m[1:2]) + m[0:1]
        h_sc[...] = h.astype(BF)

    y = _dot(h_sc[...], w_ref[...].astype(BF))
    if rope:
        @pl.when(j < 2)
        def _():
            o_ref[...] = _rope_tile(y, cos_ref[...], sa_ref[...], sb_ref[...])

        @pl.when(j >= 2)
        def _():
            o_ref[...] = y
    else:
        o_ref[...] = y


def _inproj(x, mods, layer, row0, row_step, g1, w, rope_tabs=None):
    t = x.shape[0]
    nblk = t // TOK
    ntile = w.shape[1] // D
    in_specs = [pl.BlockSpec((TOK, D), lambda i, j: (i, 0)),
                _mod_spec(layer, row0, row_step),
                pl.BlockSpec((1, D), lambda i, j: (0, 0)),
                pl.BlockSpec((D, D), lambda i, j: (0, j))]
    args = [x, mods, g1, w]
    if rope_tabs is not None:
        in_specs += [pl.BlockSpec((TOK, LANES), lambda i, j: (0, 0))] * 3
        args += list(rope_tabs)
    return pl.pallas_call(
        functools.partial(_inproj_kernel, rope=rope_tabs is not None),
        grid=(nblk, ntile),
        in_specs=in_specs,
        out_specs=pl.BlockSpec((None, TOK, D), lambda i, j: (j, i, 0)),
        out_shape=jax.ShapeDtypeStruct((ntile, t, D), F32),
        scratch_shapes=[pltpu.VMEM((TOK, D), BF)],
        compiler_params=_cp(("parallel", "arbitrary")),
        name="in_proj",
    )(*args)


def _rope_tables():
    t = np.arange(TOK)
    half = DIFF_DH // 2
    nf = half // 2
    f32 = np.float32
    inv = np.power(f32(ROPE_BASE), -np.arange(nf, dtype=f32) / f32(nf)).astype(f32)
    lane = np.arange(LANES) % DIFF_DH
    use_col = lane >= half
    fidx = (lane % half) % nf
    second = (lane % half) >= nf
    pos = np.where(use_col[None, :], (t % GRID_W).astype(f32)[:, None], (t // GRID_W).astype(f32)[:, None])
    ang = (pos * inv[fidx][None, :]).astype(f32)
    cos, sin = np.cos(ang).astype(f32), np.sin(ang).astype(f32)
    sa = np.where(second[None, :], f32(0), -sin)
    sb = np.where(second[None, :], sin, f32(0))
    return jnp.asarray(cos), jnp.asarray(sa), jnp.asarray(sb)


def _outproj_kernel(a_ref, w_ref, x_ref, m_ref, o_ref):
    y = _dot(a_ref[...].astype(BF), w_ref[...].astype(BF))
    o_ref[...] = x_ref[...] + m_ref[2:3, :] * y


def _outproj(a, w, x, mods, layer, row0, row_step):
    nblk = x.shape[0] // TOK
    return pl.pallas_call(
        _outproj_kernel,
        grid=(nblk,),
        in_specs=[pl.BlockSpec((TOK, D), lambda i: (i, 0)),
                  pl.BlockSpec((D, D), lambda i: (0, 0)),
                  pl.BlockSpec((TOK, D), lambda i: (i, 0)),
                  _mod_spec(layer, row0, row_step)],
        out_specs=pl.BlockSpec((TOK, D), lambda i: (i, 0)),
        out_shape=jax.ShapeDtypeStruct(x.shape, F32),
        compiler_params=_cp(("parallel",)),
        name="out_proj",
    )(a, w, x, mods)


HGRN_LEVELS = (1, 2, 4, 8, 16, 32, 64)


def _hgrn_masks():
    t = np.arange(CHUNK)[:, None]
    s = np.arange(CHUNK)[None, :]
    fwd = [(t == s)]
    for b in HGRN_LEVELS:
        fwd.append((t // (2 * b) == s // (2 * b)) & (t % (2 * b) >= b) & (s % (2 * b) < b))
    fwd = np.stack(fwd).astype(np.float32)
    bwd = np.transpose(fwd, (0, 2, 1))
    tri = np.stack([(s <= t), (s >= t)]).astype(np.float32)
    return jnp.asarray(fwd), jnp.asarray(bwd), jnp.asarray(tri, dtype=BF)


def _neg_abs(x):
    bits = lax.bitcast_convert_type(x, jnp.uint32) | jnp.uint32(0x80000000)
    return lax.bitcast_convert_type(bits, F32)


def _hgrn_chunks(chains):
    n = range(len(chains))
    q = [c[0] for c in chains]
    masks = [c[5] for c in chains]
    rev = [c[7] for c in chains]
    f = [c[3] + (1.0 - c[3]) * _sigmoid(c[2]) for c in chains]
    lf = [jnp.log2(x) for x in f]
    k = [1.0 - x for x in f]
    l1 = [x.astype(BF) for x in lf]
    r1 = [lf[i] - l1[i].astype(F32) for i in n]
    l2 = [x.astype(BF) for x in r1]
    l3 = [(r1[i] - l2[i].astype(F32)).astype(BF) for i in n]
    cum = [_dot(chains[i][4], l1[i]) + _dot(chains[i][4], l2[i]) + _dot(chains[i][4], l3[i]) for i in n]
    tot = [cum[i][0:1] if rev[i] else cum[i][CHUNK - 1:CHUNK] for i in n]
    qb = [x.astype(BF) for x in q]
    kb = [x.astype(BF) for x in k]

    def level(idx, e, s):
        eb = [x.astype(BF) for x in e]
        sc = [_dot_nt(qb[i] * eb[i], kb[i] * eb[i]) for i in n]
        return [s[i] + masks[i][idx] * sc[i] for i in n]

    s = [masks[i][0] * _dot_nt(qb[i], kb[i]) for i in n]

    pos = lax.broadcasted_iota(jnp.int32, (CHUNK, HD), 0)
    odd = (pos & 1) == 1
    s = level(1, [jnp.where(odd, 1.0, f[i]) if rev[i] else jnp.where(odd, f[i], 1.0) for i in n], s)

    p4 = pos & 3
    ex = []
    for i in n:
        lfn = pltpu.roll(lf[i], CHUNK - 1, axis=0)
        lfp = pltpu.roll(lf[i], 1, axis=0)
        if rev[i]:
            ex.append(jnp.where(p4 == 0, lf[i] + lfn,
                                jnp.where(p4 == 1, lf[i], jnp.where(p4 == 2, 0.0, lfp))))
        else:
            ex.append(jnp.where(p4 == 0, lfn,
                                jnp.where(p4 == 1, 0.0, jnp.where(p4 == 2, lf[i], lf[i] + lfp))))
    s = level(2, [jnp.exp2(x) for x in ex], s)

    for li, b in enumerate(HGRN_LEVELS[2:]):
        e = []
        for i in n:
            c3 = cum[i].reshape(CHUNK // (2 * b), 2 * b, HD)
            ridx = b if rev[i] else b - 1
            e.append(jnp.exp2(_neg_abs(c3 - c3[:, ridx:ridx + 1, :])).reshape(CHUNK, HD))
        s = level(3 + li, e, s)

    vb = [c[1].astype(BF) for c in chains]
    qh = [(q[i] * jnp.exp2(cum[i])).astype(BF) for i in n]
    kh = [(k[i] * jnp.exp2(tot[i] - cum[i])).astype(BF) for i in n]
    o = [_dot(s[i].astype(BF), vb[i]) + _dot_nt(qh[i], chains[i][6].astype(BF)) for i in n]
    st_new = [chains[i][6] * jnp.exp2(tot[i]) + _dot_tn(vb[i], kh[i]) for i in n]
    return list(zip(o, st_new))


def _hgrn_kernel(*refs, n_chunks, hb, unroll, has_s0, emit_state):
    q_ref, v_ref, g_ref, zf_ref, zb_ref, lb_ref, ng_ref, mf_ref, mb_ref, tri_ref = refs[:10]
    rest = refs[10:]
    if has_s0:
        s0_ref, rest = rest[0], rest[1:]
    o_ref, rest = rest[0], rest[1:]
    if emit_state:
        st_ref, rest = rest[0], rest[1:]
    of_sc, ob_sc, st_sc = rest

    for hh in range(hb):
        for d in range(2):
            st_sc[hh, d] = s0_ref[d, hh].T if has_s0 else jnp.zeros((HD, HD), F32)

    def body(i, carry):
        cf = pl.multiple_of(i * CHUNK, CHUNK)
        cb = pl.multiple_of((n_chunks - 1 - i) * CHUNK, CHUNK)
        chains = []
        for hh in range(hb):
            ln = slice(hh * HD, (hh + 1) * HD)
            chains.append((q_ref[pl.ds(cf, CHUNK), ln], v_ref[pl.ds(cf, CHUNK), ln], zf_ref[pl.ds(cf, CHUNK), ln],
                           lb_ref[hh, 0:1, :], tri_ref[0], mf_ref, st_sc[hh, 0], False))
            chains.append((q_ref[pl.ds(cb, CHUNK), ln], v_ref[pl.ds(cb, CHUNK), ln], zb_ref[pl.ds(cb, CHUNK), ln],
                           lb_ref[hh, 1:2, :], tri_ref[1], mb_ref, st_sc[hh, 1], True))
        res = _hgrn_chunks(chains)
        for hh in range(hb):
            ln = slice(hh * HD, (hh + 1) * HD)
            (o_f, stf), (o_b, stb) = res[2 * hh], res[2 * hh + 1]
            of_sc[pl.ds(cf, CHUNK), ln] = o_f
            ob_sc[pl.ds(cb, CHUNK), ln] = o_b
            st_sc[hh, 0] = stf
            st_sc[hh, 1] = stb
        return carry

    lax.fori_loop(0, n_chunks, body, 0, unroll=unroll)
    for hh in range(hb):
        ln = slice(hh * HD, (hh + 1) * HD)
        if emit_state:
            st_ref[0, hh] = st_sc[hh, 0].T
            st_ref[1, hh] = st_sc[hh, 1].T
        g = g_ref[:, ln]
        o_ref[:, ln] = _rms(of_sc[:, ln] + ob_sc[:, ln], ng_ref[...]) * (g * _sigmoid(g))


def _hgrn_scan(proj, n_seq, lb, ng, consts, hb, unroll, s0=None, emit_state=False):
    t = proj.shape[1]
    nb = t // n_seq
    mf, mb, tri = consts
    w = hb * HD

    def pspec(k):
        return pl.BlockSpec((None, n_seq, w), lambda b, h, k=k: (k, b, h))

    c3 = lambda b, h: (0, 0, 0)
    st_spec = pl.BlockSpec((None, 2, hb, HD, HD), lambda b, h: (b, 0, h, 0, 0))
    in_specs = [pspec(0), pspec(1), pspec(2), pspec(3), pspec(4),
                pl.BlockSpec((hb, 2, HD), lambda b, h: (h, 0, 0)),
                pl.BlockSpec((1, HD), lambda b, h: (0, 0)),
                pl.BlockSpec(mf.shape, c3), pl.BlockSpec(mb.shape, c3), pl.BlockSpec(tri.shape, c3)]
    args = [proj, proj, proj, proj, proj, lb, ng, mf, mb, tri]
    if s0 is not None:
        in_specs.append(st_spec)
        args.append(s0)
    out_specs = [pl.BlockSpec((n_seq, w), lambda b, h: (b, h))]
    out_shape = [jax.ShapeDtypeStruct((t, D), F32)]
    if emit_state:
        out_specs.append(st_spec)
        out_shape.append(jax.ShapeDtypeStruct((nb, 2, HEADS, HD, HD), F32))
    res = pl.pallas_call(
        functools.partial(_hgrn_kernel, n_chunks=n_seq // CHUNK, hb=hb, unroll=unroll, has_s0=s0 is not None,
                          emit_state=emit_state),
        grid=(nb, HEADS // hb),
        in_specs=in_specs,
        out_specs=out_specs,
        out_shape=out_shape,
        scratch_shapes=[pltpu.VMEM((n_seq, w), F32), pltpu.VMEM((n_seq, w), F32),
                        pltpu.VMEM((hb, 2, HD, HD), F32)],
        compiler_params=_cp(("parallel", "parallel")),
        name="hgrn_scan",
    )(*args)
    return res


def _attn_kernel(*refs, lam_init, hb, has_cache):
    lam_ref, q_ref = refs[:2]
    if has_cache:
        kc_ref, vc_ref, kn_ref, vn_ref, sg_ref, o_ref = refs[2:]
        k_refs, v_refs = (kc_ref, kn_ref), (vc_ref, vn_ref)
    else:
        kn_ref, vn_ref, sg_ref, o_ref = refs[2:]
        k_refs, v_refs = (kn_ref,), (vn_ref,)

    lp = lam_ref[...]
    lam = (jnp.exp(jnp.sum(lp[0:1] * lp[1:2], axis=-1, keepdims=True))
           - jnp.exp(jnp.sum(lp[2:3] * lp[3:4], axis=-1, keepdims=True)) + lam_init)
    first = lax.broadcasted_iota(jnp.int32, (1, HD), 1) < DIFF_DH

    qscale = DIFF_DH ** -0.5 * math.log2(math.e)
    for hh in range(hb):
        ln = slice(hh * HD, (hh + 1) * HD)
        q = q_ref[:, ln] * qscale
        qs = (jnp.where(first, q, 0.0).astype(BF), jnp.where(first, 0.0, q).astype(BF))
        ks = [r[:, ln].astype(BF) for r in k_refs]
        vs = [r[:, ln].astype(BF) for r in v_refs]
        outs = []
        for qm in qs:
            sc = [_dot_nt(qm, kk) for kk in ks]
            mx = functools.reduce(jnp.maximum, [jnp.max(s, axis=-1, keepdims=True) for s in sc])
            es = [jnp.exp2(s - mx) for s in sc]
            den = functools.reduce(jnp.add, [jnp.sum(e, axis=-1, keepdims=True) for e in es])
            pv = functools.reduce(jnp.add, [_dot(e.astype(BF), vv) for e, vv in zip(es, vs)])
            outs.append(pv * (1.0 / den))
        o = outs[0] - lam * outs[1]
        o_ref[:, ln] = _rms(o, sg_ref[...]) * (1.0 - lam_init)


def _attention(qkv, n_seq, lam_p, subln, lam_init, hb, cache=None):
    t = qkv.shape[1]
    nb = t // n_seq
    nqb = n_seq // ATT_QB
    w = hb * HD
    in_specs = [pl.BlockSpec((4, DIFF_DH), lambda b, h, i: (0, 0)),
                pl.BlockSpec((None, ATT_QB, w), lambda b, h, i: (0, b * nqb + i, h))]
    args = [lam_p, qkv]
    if cache is not None:
        past = cache[0].shape[1]
        in_specs += [pl.BlockSpec((None, past, w), lambda b, h, i: (b, 0, h))] * 2
        args += [cache[0], cache[1]]
    in_specs += [pl.BlockSpec((None, n_seq, w), lambda b, h, i: (1, b, h)),
                 pl.BlockSpec((None, n_seq, w), lambda b, h, i: (2, b, h)),
                 pl.BlockSpec((1, HD), lambda b, h, i: (0, 0))]
    args += [qkv, qkv, subln]
    return pl.pallas_call(
        functools.partial(_attn_kernel, lam_init=lam_init, hb=hb, has_cache=cache is not None),
        grid=(nb, HEADS // hb, nqb),
        in_specs=in_specs,
        out_specs=pl.BlockSpec((ATT_QB, w), lambda b, h, i: (b * nqb + i, h)),
        out_shape=jax.ShapeDtypeStruct((t, D), F32),
        compiler_params=_cp(("parallel", "parallel", "arbitrary")),
        name="diff_attention",
    )(*args)


def _shift_tokens(u, d, first, cols):
    n = u.shape[0]
    r = pltpu.roll(u, (-d) % n, axis=0)
    row = lax.broadcasted_iota(jnp.int32, (SUBLANES, u.shape[1]), 0)
    edge = (row < -d) if d < 0 else (row >= SUBLANES - d)
    want = 0 if d < 0 else cols - SUBLANES
    pieces, prev = [], 0
    for g in range(0, n, SUBLANES):
        if (first + g) % cols == want:
            if g > prev:
                pieces.append(r[prev:g])
            pieces.append(jnp.where(edge, 0.0, r[g:g + SUBLANES]))
            prev = g + SUBLANES
    if prev < n:
        pieces.append(r[prev:])
    return jnp.concatenate(pieces, axis=0) if len(pieces) > 1 else pieces[0]


def _dwconv_rows(u_ref, r0, n, cw, cb, cols, two_d):
    halo = cols if two_d else SUBLANES
    lo, hi = r0 - halo, r0 + n + halo
    zero = jnp.zeros((halo, u_ref.shape[1]), F32)
    ext = jnp.concatenate(([zero] if lo < 0 else []) + [u_ref[max(lo, 0):min(hi, TOK), :]]
                          + ([zero] if hi > TOK else []), axis=0)
    ul = _shift_tokens(ext, -1, lo, cols)
    ur = _shift_tokens(ext, 1, lo, cols)

    def taps(k, a):
        return cw[k:k + 1] * ul[a:a + n] + cw[k + 1:k + 2] * ext[a:a + n] + cw[k + 2:k + 3] * ur[a:a + n]

    out = taps(3, halo) + cb
    if two_d:
        out = out + taps(0, 0) + taps(6, 2 * halo)
    return out


def _ffn_kernel(*refs, cols, two_d, final):
    x_ref, m_ref, g_ref, wa_ref, wv_ref, cwa_ref, cwv_ref, cba_ref, cbv_ref, wdn_ref = refs[:10]
    rest = refs[10:]
    if final:
        fg_ref, rest = rest[0], rest[1:]
    o_ref, h_sc, acc_sc, ua0_sc, uv0_sc, ua1_sc, uv1_sc = rest
    j = pl.program_id(1)

    @pl.when(j == 0)
    def _():
        m = m_ref[...]
        h = _rms(x_ref[...], g_ref[...]) * (1.0 + m[4:5]) + m[3:4]
        h_sc[...] = h.astype(BF)
        acc_sc[...] = jnp.zeros_like(acc_sc)
        ua1_sc[...] = jnp.zeros_like(ua1_sc)
        uv1_sc[...] = jnp.zeros_like(uv1_sc)

    def step(ua_w, uv_w, ua_r, uv_r):
        wa = wa_ref[...].astype(BF)
        wv = wv_ref[...].astype(BF)
        wd = wdn_ref[...].astype(BF)
        cwa, cwv, cba, cbv = cwa_ref[...], cwv_ref[...], cba_ref[...], cbv_ref[...]
        keep = jnp.where(j > 0, 1.0, 0.0).astype(F32)
        nrow = FFN_ROWS_2D if two_d else FFN_ROWS_1D
        for r0 in range(0, TOK, nrow):
            rows = slice(r0, r0 + nrow)
            h = h_sc[rows, :]
            ua_w[rows, :] = _dot(h, wa)
            uv_w[rows, :] = _dot(h, wv)
            a = _dwconv_rows(ua_r, r0, nrow, cwa, cba, cols, two_d)
            v = _dwconv_rows(uv_r, r0, nrow, cwv, cbv, cols, two_d)
            act = a * _sigmoid(a) * (v * keep)
            acc_sc[rows, :] += _dot(act.astype(BF), wd)

    @pl.when(j % 2 == 0)
    def _():
        step(ua0_sc, uv0_sc, ua1_sc, uv1_sc)

    @pl.when(j % 2 == 1)
    def _():
        step(ua1_sc, uv1_sc, ua0_sc, uv0_sc)

    @pl.when(j == N_FF_TILES)
    def _():
        y = x_ref[...] + m_ref[5:6, :] * acc_sc[...]
        if final:
            y = _rms(y, fg_ref[...])
        o_ref[...] = y


def _ffn(x, mods, layer, row0, row_step, g2, wup, cw, cb, wdn, cols, two_d, final_g=None):
    nblk = x.shape[0] // TOK
    nt = N_FF_TILES
    up = lambda j: jnp.minimum(j, nt - 1)
    dn = lambda j: jnp.maximum(j - 1, 0)
    in_specs = [pl.BlockSpec((TOK, D), lambda i, j: (i, 0)),
                _mod_spec(layer, row0, row_step),
                pl.BlockSpec((1, D), lambda i, j: (0, 0)),
                pl.BlockSpec((None, D, FF_TILE), lambda i, j: (layer, 0, up(j))),
                pl.BlockSpec((None, D, FF_TILE), lambda i, j: (layer, 0, nt + up(j))),
                pl.BlockSpec((None, 9, FF_TILE), lambda i, j: (layer, 0, dn(j))),
                pl.BlockSpec((None, 9, FF_TILE), lambda i, j: (layer, 0, nt + dn(j))),
                pl.BlockSpec((None, 1, FF_TILE), lambda i, j: (layer, 0, dn(j))),
                pl.BlockSpec((None, 1, FF_TILE), lambda i, j: (layer, 0, nt + dn(j))),
                pl.BlockSpec((None, FF_TILE, D), lambda i, j: (layer, dn(j), 0))]
    args = [x, mods, g2, wup, wup, cw, cw, cb, cb, wdn]
    if final_g is not None:
        in_specs.append(pl.BlockSpec((1, D), lambda i, j: (0, 0)))
        args.append(final_g)
    return pl.pallas_call(
        functools.partial(_ffn_kernel, cols=cols, two_d=two_d, final=final_g is not None),
        grid=(nblk, N_FF_TILES + 1),
        in_specs=in_specs,
        out_specs=pl.BlockSpec((TOK, D), lambda i, j: (i, 0)),
        out_shape=jax.ShapeDtypeStruct(x.shape, F32),
        scratch_shapes=[pltpu.VMEM((TOK, D), BF), pltpu.VMEM((TOK, D), F32),
                        *[pltpu.VMEM((TOK, FF_TILE), F32) for _ in range(4)]],
        compiler_params=_cp(("parallel", "arbitrary")),
        name="conv_ffn",
    )(*args)


def kernel(x_prompt, x_sample, state_hgrn, cache_k, cache_v, c, c_ctx, ada_w, ada_b, norm1, norm2, final_norm,
           pool_w, pool_scale, hgrn_w_in, hgrn_lb, hgrn_norm, hgrn_w_out, diff_w_in, diff_lambda, diff_subln,
           diff_w_out, ffn_w_up, ffn_conv_w, ffn_conv_b, ffn_w_down):
    nbp, seq, _ = x_prompt.shape
    nbs, dseq, _ = x_sample.shape
    assert seq * (TOK // seq) == TOK and dseq == TOK and D_FF == ffn_w_down.shape[1]
    rows = dseq // GRID_W

    cond = jnp.concatenate([c_ctx[None, :], c, jnp.zeros((8 - 1 - nbs, D), F32)], axis=0)
    mods = _modulation(cond.T, ada_w, ada_b).reshape(DEPTH, 8, 6, D)

    groups = [
        dict(x=x_prompt.reshape(nbp * seq, D), row0=0, step=0, n_seq=seq, two_d=False, cols=seq,
             pool=_pool_counts(1, seq, TOK // seq)),
        dict(x=x_sample.reshape(nbs * dseq, D), row0=1, step=1, n_seq=dseq, two_d=True, cols=GRID_W,
             pool=_pool_counts(rows, GRID_W, 1)),
    ]
    conv_w = ffn_conv_w.reshape(DEPTH, 9, 2 * D_FF)
    conv_b = ffn_conv_b.reshape(DEPTH, 1, 2 * D_FF)
    hgrn_consts = _hgrn_masks()
    rope_tabs = _rope_tables()

    lbs = jax.nn.softmax(hgrn_lb.astype(F32), axis=1)
    lbs = jnp.cumsum(lbs, axis=1) - lbs[:, :1]

    new_state = new_k = new_v = None
    for i in range(DEPTH):
        kind, j = i % 3, i // 3
        g1 = norm1[i][None, :]
        g2 = norm2[i][None, :]
        for gi, gr in enumerate(groups):
            x, row0, step = gr["x"], gr["row0"], gr["step"]
            is_prompt = gi == 0
            if kind == 0:
                x = _pool_layer(x, mods, i, row0, step, g1, gr["pool"], pool_w[j], pool_scale[j][None, :],
                                gr["cols"], gr["two_d"])
            elif kind == 1:
                proj = _inproj(x, mods, i, row0, step, g1, hgrn_w_in[j])
                lb = lbs[:, i].reshape(2, HEADS, HD).transpose(1, 0, 2)
                ng = hgrn_norm[j][None, :]
                if is_prompt:
                    og, new_state = _hgrn_scan(proj, gr["n_seq"], lb, ng, hgrn_consts, HGRN_HEADS_PROMPT, True,
                                               None, True)
                else:
                    (og,) = _hgrn_scan(proj, gr["n_seq"], lb, ng, hgrn_consts, HGRN_HEADS_SAMPLE,
                                       HGRN_UNROLL_SAMPLE, state_hgrn[:, j], False)
                x = _outproj(og, hgrn_w_out[j], x, mods, i, row0, step)
            else:
                lam_init = 0.8 - 0.6 * math.exp(-0.3 * i)
                if is_prompt:
                    qkv = _inproj(x, mods, i, row0, step, g1, diff_w_in[j])
                    new_k, new_v = qkv[1], qkv[2]
                    cache = None
                else:
                    qkv = _inproj(x, mods, i, row0, step, g1, diff_w_in[j], rope_tabs)
                    cache = (cache_k[:, j].reshape(nbs, -1, D), cache_v[:, j].reshape(nbs, -1, D))
                og = _attention(qkv, gr["n_seq"], diff_lambda[j], diff_subln[j][None, :], lam_init,
                                ATT_HEADS_PROMPT if is_prompt else ATT_HEADS_SAMPLE, cache)
                x = _outproj(og, diff_w_out[j], x, mods, i, row0, step)
            fg = final_norm[None, :] if i == DEPTH - 1 else None
            gr["x"] = _ffn(x, mods, i, row0, step, g2, ffn_w_up, conv_w, conv_b, ffn_w_down,
                           gr["cols"], gr["two_d"], fg)

    y_prompt = groups[0]["x"].reshape(nbp, seq, D)
    y_sample = groups[1]["x"].reshape(nbs, dseq, D)
    new_state_hgrn = new_state[:, None]
    new_cache_k = new_k.reshape(nbp, 1, seq, HEADS, 2, DIFF_DH)
    new_cache_v = new_v.reshape(nbp, 1, seq, HEADS, 2 * DIFF_DH)
    return (y_prompt, y_sample, new_state_hgrn, new_cache_k, new_cache_v)
```

```python
import functools
import math

import numpy as np
import jax
import jax.numpy as jnp
from jax import lax
from jax.experimental import pallas as pl
from jax.experimental.pallas import tpu as pltpu

F32 = jnp.float32
BF = jnp.bfloat16

D = 1024
DEPTH = 4
GRID_W = 64
POOL_WINDOWS = (2, 4, 8, 16)
POOL_GROUP = D // len(POOL_WINDOWS)
HEADS = 8
HD = 128
DIFF_DH = 64
ROPE_BASE = 10000.0
D_FF = 2816
EPS = 1e-6

LANES = 128
SUBLANES = 8
TOK = 1024
FF_TILE = 256
FFN_ROWS_1D = 128
FFN_ROWS_2D = 256
N_FF_TILES = D_FF // FF_TILE
CHUNK = 128
HGRN_HEADS_PROMPT = 2
HGRN_HEADS_SAMPLE = 2
HGRN_UNROLL_SAMPLE = 1
ATT_QB = 256
ATT_HEADS_PROMPT = 8
ATT_HEADS_SAMPLE = 1
MOD_TN = 1536
MOD_LC = 768
VMEM_LIMIT = 56 * 1024 * 1024


def _cp(sem, vmem=VMEM_LIMIT):
    return pltpu.CompilerParams(dimension_semantics=sem, vmem_limit_bytes=vmem)


def _sigmoid(x):
    return 1.0 / (1.0 + jnp.exp(-x))


def _rms(x, g):
    return x * lax.rsqrt(jnp.mean(x * x, axis=-1, keepdims=True) + EPS) * g


def _dot(a, b):
    return jnp.dot(a, b, preferred_element_type=F32)


def _dot_nt(a, b):
    return lax.dot_general(a, b, (((1,), (1,)), ((), ())), preferred_element_type=F32)


def _dot_tn(a, b):
    return lax.dot_general(a, b, (((0,), (0,)), ((), ())), preferred_element_type=F32)


def _mod_kernel(ct_ref, w_ref, b_ref, o_ref, sb_ref):
    @pl.when((pl.program_id(0) == 0) & (pl.program_id(1) == 0))
    def _():
        ct = ct_ref[...]
        s = ct * _sigmoid(ct)
        for r in range(3):
            sb_ref[r] = jnp.broadcast_to(s[:, r:r + 1], (D, LANES))

    for lc in range(MOD_TN // MOD_LC):
        c0 = lc * MOD_LC

        def body(i, accs, c0=c0):
            k0 = pl.multiple_of(i * SUBLANES, SUBLANES)
            wv = w_ref[pl.ds(k0, SUBLANES), c0:c0 + MOD_LC]
            out = []
            for r in range(3):
                sv = sb_ref[r, pl.ds(k0, SUBLANES), :]
                out.append(accs[r] + wv * jnp.tile(sv, (1, MOD_LC // LANES)))
            return tuple(out)

        accs = lax.fori_loop(0, D // SUBLANES, body,
                             tuple(jnp.zeros((SUBLANES, MOD_LC), F32) for _ in range(3)), unroll=4)
        for r in range(3):
            o_ref[r:r + 1, c0:c0 + MOD_LC] = (jnp.sum(accs[r], axis=0, keepdims=True)
                                              + b_ref[:, c0:c0 + MOD_LC])
        o_ref[3:8, c0:c0 + MOD_LC] = jnp.zeros((5, MOD_LC), F32)


def _modulation(cond_t, ada_w, ada_b):
    return pl.pallas_call(
        _mod_kernel,
        grid=(DEPTH, 6 * D // MOD_TN),
        in_specs=[pl.BlockSpec((D, 8), lambda l, j: (0, 0)),
                  pl.BlockSpec((None, D, MOD_TN), lambda l, j: (l, 0, j)),
                  pl.BlockSpec((None, 1, MOD_TN), lambda l, j: (l, 0, j))],
        out_specs=pl.BlockSpec((None, 8, MOD_TN), lambda l, j: (l, 0, j)),
        out_shape=jax.ShapeDtypeStruct((DEPTH, 8, 6 * D), F32),
        scratch_shapes=[pltpu.VMEM((3, D, LANES), F32)],
        compiler_params=_cp(("arbitrary", "arbitrary")),
        name="modulation",
    )(cond_t, ada_w, ada_b.reshape(DEPTH, 1, 6 * D))


def _mod_spec(layer, row0, row_step):
    return pl.BlockSpec((None, None, 6, D), lambda i, *_: (layer, row0 + row_step * i, 0, 0))


def _shift_grid_rows(a, n):
    z = jnp.zeros((abs(n), a.shape[1]), a.dtype)
    if n > 0:
        return jnp.concatenate([a[n:], z], axis=0)
    return jnp.concatenate([z, a[:n]], axis=0)


def _box_sum(x, w, shift):
    fwd = bwd = x
    k = 1
    while k < w // 2:
        fwd = fwd + shift(fwd, k)
        bwd = bwd + shift(bwd, -k)
        k *= 2
    return shift(bwd, -1) + fwd


def _pool_kernel(x_ref, m_ref, g_ref, ic_ref, w_ref, sc_ref, o_ref, *, cols, two_d):
    x = x_ref[...]
    m = m_ref[...]
    h = _rms(x, g_ref[...]) * (1.0 + m[1:2]) + m[0:1]
    for g, w in enumerate(POOL_WINDOWS):
        sl = slice(g * POOL_GROUP, (g + 1) * POOL_GROUP)
        hg = h[:, sl]
        s = _box_sum(hg, w, lambda a, d: _shift_tokens(a, d, 0, cols))
        if two_d:
            s = _box_sum(s, w, lambda a, d: _shift_grid_rows(a, d * cols))
        ic = ic_ref[g]
        d = s * jnp.concatenate([ic, ic], axis=1) - hg
        y = _dot(d.astype(BF), w_ref[g].astype(BF)) * sc_ref[:, sl]
        o_ref[:, sl] = x[:, sl] + m[2:3, sl] * y


def _pool_counts(rows, cols, segs):
    n = rows * cols
    r = np.arange(n) // cols
    c = np.arange(n) % cols
    invs = []
    for w in POOL_WINDOWS:
        def size(pos, length):
            return np.clip(pos + w // 2, 0, length) - np.clip(pos - w // 2, 0, length)
        cnt = (size(np.arange(rows), rows)[r] * size(np.arange(cols), cols)[c]).astype(np.float64)
        invs.append(np.tile(1.0 / cnt, segs))
    return jnp.asarray(np.broadcast_to(np.stack(invs)[:, :, None], (len(POOL_WINDOWS), n * segs, LANES)),
                       dtype=F32)


def _pool_layer(x, mods, layer, row0, row_step, g1, inv, w, scale, cols, two_d):
    nblk = x.shape[0] // TOK
    const3 = lambda i: (0, 0, 0)
    return pl.pallas_call(
        functools.partial(_pool_kernel, cols=cols, two_d=two_d),
        grid=(nblk,),
        in_specs=[pl.BlockSpec((TOK, D), lambda i: (i, 0)),
                  _mod_spec(layer, row0, row_step),
                  pl.BlockSpec((1, D), lambda i: (0, 0)),
                  pl.BlockSpec((4, TOK, LANES), const3),
                  pl.BlockSpec((4, POOL_GROUP, POOL_GROUP), const3),
                  pl.BlockSpec((1, D), lambda i: (0, 0))],
        out_specs=pl.BlockSpec((TOK, D), lambda i: (i, 0)),
        out_shape=jax.ShapeDtypeStruct(x.shape, F32),
        compiler_params=_cp(("parallel",)),
        name="pool_mixer",
    )(x, mods, g1, inv, w, scale)


def _rope_tile(y, cos, sa, sb):
    outs = []
    for c in range(D // LANES):
        yc = y[:, c * LANES:(c + 1) * LANES]
        up = pltpu.roll(yc, LANES - 16, axis=1)
        dn = pltpu.roll(yc, 16, axis=1)
        outs.append(yc * cos + up * sa + dn * sb)
    return jnp.concatenate(outs, axis=1)


def _inproj_kernel(x_ref, m_ref, g_ref, w_ref, *rest, rope):
    if rope:
        cos_ref, sa_ref, sb_ref, o_ref, h_sc, w_sc = rest
    else:
        o_ref, h_sc, w_sc = rest
    j = pl.program_id(0)
    i = pl.program_id(1)

    @pl.when(j == 0)
    def _():
        m = m_ref[...]
        h = _rms(x_ref[...], g_ref[...]) * (1.0 + m[1:2]) + m[0:1]
        h_sc[i] = h.astype(BF)

    @pl.when(i == 0)
    def _():
        w_sc[...] = w_ref[...].astype(BF)

    y = _dot(h_sc[i], w_sc[...])
    if rope:
        @pl.when(j < 2)
        def _():
            o_ref[...] = _rope_tile(y, cos_ref[...], sa_ref[...], sb_ref[...])

        @pl.when(j >= 2)
        def _():
            o_ref[...] = y
    else:
        o_ref[...] = y


def _inproj(x, mods, layer, row0, row_step, g1, w, rope_tabs=None):
    t = x.shape[0]
    nblk = t // TOK
    ntile = w.shape[1] // D
    in_specs = [pl.BlockSpec((TOK, D), lambda j, i: (jnp.where(j == 0, i, nblk - 1), 0)),
                pl.BlockSpec((None, None, 6, D), lambda j, i: (layer, row0 + row_step * i, 0, 0)),
                pl.BlockSpec((1, D), lambda j, i: (0, 0)),
                pl.BlockSpec((D, D), lambda j, i: (0, j))]
    args = [x, mods, g1, w]
    if rope_tabs is not None:
        in_specs += [pl.BlockSpec((TOK, LANES), lambda j, i: (0, 0))] * 3
        args += list(rope_tabs)
    return pl.pallas_call(
        functools.partial(_inproj_kernel, rope=rope_tabs is not None),
        grid=(ntile, nblk),
        in_specs=in_specs,
        out_specs=pl.BlockSpec((None, TOK, D), lambda j, i: (j, i, 0)),
        out_shape=jax.ShapeDtypeStruct((ntile, t, D), F32),
        scratch_shapes=[pltpu.VMEM((nblk, TOK, D), BF), pltpu.VMEM((D, D), BF)],
        compiler_params=_cp(("arbitrary", "arbitrary")),
        name="in_proj",
    )(*args)


def _rope_tables():
    t = np.arange(TOK)
    half = DIFF_DH // 2
    nf = half // 2
    f32 = np.float32
    inv = np.power(f32(ROPE_BASE), -np.arange(nf, dtype=f32) / f32(nf)).astype(f32)
    lane = np.arange(LANES) % DIFF_DH
    use_col = lane >= half
    fidx = (lane % half) % nf
    second = (lane % half) >= nf
    pos = np.where(use_col[None, :], (t % GRID_W).astype(f32)[:, None], (t // GRID_W).astype(f32)[:, None])
    ang = (pos * inv[fidx][None, :]).astype(f32)
    cos, sin = np.cos(ang).astype(f32), np.sin(ang).astype(f32)
    sa = np.where(second[None, :], f32(0), -sin)
    sb = np.where(second[None, :], sin, f32(0))
    return jnp.asarray(cos), jnp.asarray(sa), jnp.asarray(sb)


def _outproj_kernel(a_ref, w_ref, x_ref, m_ref, o_ref):
    y = _dot(a_ref[...].astype(BF), w_ref[...].astype(BF))
    o_ref[...] = x_ref[...] + m_ref[2:3, :] * y


def _outproj(a, w, x, mods, layer, row0, row_step):
    nblk = x.shape[0] // TOK
    return pl.pallas_call(
        _outproj_kernel,
        grid=(nblk,),
        in_specs=[pl.BlockSpec((TOK, D), lambda i: (i, 0)),
                  pl.BlockSpec((D, D), lambda i: (0, 0)),
                  pl.BlockSpec((TOK, D), lambda i: (i, 0)),
                  _mod_spec(layer, row0, row_step)],
        out_specs=pl.BlockSpec((TOK, D), lambda i: (i, 0)),
        out_shape=jax.ShapeDtypeStruct(x.shape, F32),
        compiler_params=_cp(("parallel",)),
        name="out_proj",
    )(a, w, x, mods)


HGRN_LEVELS = (1, 2, 4, 8, 16, 32, 64)


def _hgrn_masks():
    t = np.arange(CHUNK)[:, None]
    s = np.arange(CHUNK)[None, :]
    fwd = [(t == s)]
    for b in HGRN_LEVELS:
        fwd.append((t // (2 * b) == s // (2 * b)) & (t % (2 * b) >= b) & (s % (2 * b) < b))
    fwd = np.stack(fwd).astype(np.float32)
    bwd = np.transpose(fwd, (0, 2, 1))
    tri = np.stack([(s <= t), (s >= t)]).astype(np.float32)
    return jnp.asarray(fwd), jnp.asarray(bwd), jnp.asarray(tri, dtype=BF)


def _neg_abs(x):
    bits = lax.bitcast_convert_type(x, jnp.uint32) | jnp.uint32(0x80000000)
    return lax.bitcast_convert_type(bits, F32)


def _hgrn_chunks(chains):
    n = range(len(chains))
    q = [c[0] for c in chains]
    masks = [c[5] for c in chains]
    rev = [c[7] for c in chains]
    f = [c[3] + (1.0 - c[3]) * _sigmoid(c[2]) for c in chains]
    lf = [jnp.log2(x) for x in f]
    k = [1.0 - x for x in f]
    l1 = [x.astype(BF) for x in lf]
    r1 = [lf[i] - l1[i].astype(F32) for i in n]
    l2 = [x.astype(BF) for x in r1]
    l3 = [(r1[i] - l2[i].astype(F32)).astype(BF) for i in n]
    cum = [_dot(chains[i][4], l1[i]) + _dot(chains[i][4], l2[i]) + _dot(chains[i][4], l3[i]) for i in n]
    tot = [cum[i][0:1] if rev[i] else cum[i][CHUNK - 1:CHUNK] for i in n]
    qb = [x.astype(BF) for x in q]
    kb = [x.astype(BF) for x in k]

    def level(idx, e, s):
        eb = [x.astype(BF) for x in e]
        sc = [_dot_nt(qb[i] * eb[i], kb[i] * eb[i]) for i in n]
        return [s[i] + masks[i][idx] * sc[i] for i in n]

    s = [masks[i][0] * _dot_nt(qb[i], kb[i]) for i in n]

    pos = lax.broadcasted_iota(jnp.int32, (CHUNK, HD), 0)
    odd = (pos & 1) == 1
    s = level(1, [jnp.where(odd, 1.0, f[i]) if rev[i] else jnp.where(odd, f[i], 1.0) for i in n], s)

    p4 = pos & 3
    ex = []
    for i in n:
        lfn = pltpu.roll(lf[i], CHUNK - 1, axis=0)
        lfp = pltpu.roll(lf[i], 1, axis=0)
        if rev[i]:
            ex.append(jnp.where(p4 == 0, lf[i] + lfn,
                                jnp.where(p4 == 1, lf[i], jnp.where(p4 == 2, 0.0, lfp))))
        else:
            ex.append(jnp.where(p4 == 0, lfn,
                                jnp.where(p4 == 1, 0.0, jnp.where(p4 == 2, lf[i], lf[i] + lfp))))
    s = level(2, [jnp.exp2(x) for x in ex], s)

    for li, b in enumerate(HGRN_LEVELS[2:]):
        e = []
        for i in n:
            c3 = cum[i].reshape(CHUNK // (2 * b), 2 * b, HD)
            ridx = b if rev[i] else b - 1
            e.append(jnp.exp2(_neg_abs(c3 - c3[:, ridx:ridx + 1, :])).reshape(CHUNK, HD))
        s = level(3 + li, e, s)

    vb = [c[1].astype(BF) for c in chains]
    qh = [(q[i] * jnp.exp2(cum[i])).astype(BF) for i in n]
    kh = [(k[i] * jnp.exp2(tot[i] - cum[i])).astype(BF) for i in n]
    o = [_dot(s[i].astype(BF), vb[i]) + _dot_nt(qh[i], chains[i][6].astype(BF)) for i in n]
    st_new = [chains[i][6] * jnp.exp2(tot[i]) + _dot_tn(vb[i], kh[i]) for i in n]
    return list(zip(o, st_new))


def _hgrn_kernel(*refs, n_chunks, hb, unroll, has_s0, emit_state):
    q_ref, v_ref, g_ref, zf_ref, zb_ref, lb_ref, ng_ref, mf_ref, mb_ref, tri_ref = refs[:10]
    rest = refs[10:]
    if has_s0:
        s0_ref, rest = rest[0], rest[1:]
    o_ref, rest = rest[0], rest[1:]
    if emit_state:
        st_ref, rest = rest[0], rest[1:]
    of_sc, ob_sc, st_sc = rest

    for hh in range(hb):
        for d in range(2):
            st_sc[hh, d] = s0_ref[d, hh].T if has_s0 else jnp.zeros((HD, HD), F32)

    def body(i, carry):
        cf = pl.multiple_of(i * CHUNK, CHUNK)
        cb = pl.multiple_of((n_chunks - 1 - i) * CHUNK, CHUNK)
        chains = []
        for hh in range(hb):
            ln = slice(hh * HD, (hh + 1) * HD)
            chains.append((q_ref[pl.ds(cf, CHUNK), ln], v_ref[pl.ds(cf, CHUNK), ln], zf_ref[pl.ds(cf, CHUNK), ln],
                           lb_ref[hh, 0:1, :], tri_ref[0], mf_ref, st_sc[hh, 0], False))
            chains.append((q_ref[pl.ds(cb, CHUNK), ln], v_ref[pl.ds(cb, CHUNK), ln], zb_ref[pl.ds(cb, CHUNK), ln],
                           lb_ref[hh, 1:2, :], tri_ref[1], mb_ref, st_sc[hh, 1], True))
        res = _hgrn_chunks(chains)
        for hh in range(hb):
            ln = slice(hh * HD, (hh + 1) * HD)
            (o_f, stf), (o_b, stb) = res[2 * hh], res[2 * hh + 1]
            of_sc[pl.ds(cf, CHUNK), ln] = o_f
            ob_sc[pl.ds(cb, CHUNK), ln] = o_b
            st_sc[hh, 0] = stf
            st_sc[hh, 1] = stb
        return carry

    lax.fori_loop(0, n_chunks, body, 0, unroll=unroll)
    for hh in range(hb):
        ln = slice(hh * HD, (hh + 1) * HD)
        if emit_state:
            st_ref[0, hh] = st_sc[hh, 0].T
            st_ref[1, hh] = st_sc[hh, 1].T
        g = g_ref[:, ln]
        o_ref[:, ln] = (_rms(of_sc[:, ln] + ob_sc[:, ln], ng_ref[...]) * (g * _sigmoid(g))).astype(BF)


def _hgrn_scan(proj, n_seq, lb, ng, consts, hb, unroll, s0=None, emit_state=False):
    t = proj.shape[1]
    nb = t // n_seq
    mf, mb, tri = consts
    w = hb * HD

    def pspec(k):
        return pl.BlockSpec((None, n_seq, w), lambda b, h, k=k: (k, b, h))

    c3 = lambda b, h: (0, 0, 0)
    st_spec = pl.BlockSpec((None, 2, hb, HD, HD), lambda b, h: (b, 0, h, 0, 0))
    in_specs = [pspec(0), pspec(1), pspec(2), pspec(3), pspec(4),
                pl.BlockSpec((hb, 2, HD), lambda b, h: (h, 0, 0)),
                pl.BlockSpec((1, HD), lambda b, h: (0, 0)),
                pl.BlockSpec(mf.shape, c3), pl.BlockSpec(mb.shape, c3), pl.BlockSpec(tri.shape, c3)]
    args = [proj, proj, proj, proj, proj, lb, ng, mf, mb, tri]
    if s0 is not None:
        in_specs.append(st_spec)
        args.append(s0)
    out_specs = [pl.BlockSpec((n_seq, w), lambda b, h: (b, h))]
    out_shape = [jax.ShapeDtypeStruct((t, D), BF)]
    if emit_state:
        out_specs.append(st_spec)
        out_shape.append(jax.ShapeDtypeStruct((nb, 2, HEADS, HD, HD), F32))
    res = pl.pallas_call(
        functools.partial(_hgrn_kernel, n_chunks=n_seq // CHUNK, hb=hb, unroll=unroll, has_s0=s0 is not None,
                          emit_state=emit_state),
        grid=(nb, HEADS // hb),
        in_specs=in_specs,
        out_specs=out_specs,
        out_shape=out_shape,
        scratch_shapes=[pltpu.VMEM((n_seq, w), F32), pltpu.VMEM((n_seq, w), F32),
                        pltpu.VMEM((hb, 2, HD, HD), F32)],
        compiler_params=_cp(("parallel", "parallel")),
        name="hgrn_scan",
    )(*args)
    return res


def _attn_kernel(*refs, lam_init, hb, has_cache):
    lam_ref, q_ref = refs[:2]
    if has_cache:
        kc_ref, vc_ref, kn_ref, vn_ref, sg_ref, o_ref = refs[2:]
        k_refs, v_refs = (kc_ref, kn_ref), (vc_ref, vn_ref)
    else:
        kn_ref, vn_ref, sg_ref, o_ref = refs[2:]
        k_refs, v_refs = (kn_ref,), (vn_ref,)

    lp = lam_ref[...]
    lam = (jnp.exp(jnp.sum(lp[0:1] * lp[1:2], axis=-1, keepdims=True))
           - jnp.exp(jnp.sum(lp[2:3] * lp[3:4], axis=-1, keepdims=True)) + lam_init)
    first = lax.broadcasted_iota(jnp.int32, (1, HD), 1) < DIFF_DH

    qscale = DIFF_DH ** -0.5 * math.log2(math.e)
    for hh in range(hb):
        ln = slice(hh * HD, (hh + 1) * HD)
        q = q_ref[:, ln] * qscale
        qs = (jnp.where(first, q, 0.0).astype(BF), jnp.where(first, 0.0, q).astype(BF))
        ks = [r[:, ln].astype(BF) for r in k_refs]
        vs = [r[:, ln].astype(BF) for r in v_refs]
        outs = []
        for qm in qs:
            sc = [_dot_nt(qm, kk) for kk in ks]
            mx = functools.reduce(jnp.maximum, [jnp.max(s, axis=-1, keepdims=True) for s in sc])
            es = [jnp.exp2(s - mx) for s in sc]
            den = functools.reduce(jnp.add, [jnp.sum(e, axis=-1, keepdims=True) for e in es])
            pv = functools.reduce(jnp.add, [_dot(e.astype(BF), vv) for e, vv in zip(es, vs)])
            outs.append(pv * (1.0 / den))
        o = outs[0] - lam * outs[1]
        o_ref[:, ln] = (_rms(o, sg_ref[...]) * (1.0 - lam_init)).astype(BF)


def _attention(qkv, n_seq, lam_p, subln, lam_init, hb, cache=None):
    t = qkv.shape[1]
    nb = t // n_seq
    nqb = n_seq // ATT_QB
    w = hb * HD
    in_specs = [pl.BlockSpec((4, DIFF_DH), lambda b, h, i: (0, 0)),
                pl.BlockSpec((None, ATT_QB, w), lambda b, h, i: (0, b * nqb + i, h))]
    args = [lam_p, qkv]
    if cache is not None:
        past = cache[0].shape[1]
        in_specs += [pl.BlockSpec((None, past, w), lambda b, h, i: (b, 0, h))] * 2
        args += [cache[0], cache[1]]
    in_specs += [pl.BlockSpec((None, n_seq, w), lambda b, h, i: (1, b, h)),
                 pl.BlockSpec((None, n_seq, w), lambda b, h, i: (2, b, h)),
                 pl.BlockSpec((1, HD), lambda b, h, i: (0, 0))]
    args += [qkv, qkv, subln]
    return pl.pallas_call(
        functools.partial(_attn_kernel, lam_init=lam_init, hb=hb, has_cache=cache is not None),
        grid=(nb, HEADS // hb, nqb),
        in_specs=in_specs,
        out_specs=pl.BlockSpec((ATT_QB, w), lambda b, h, i: (b * nqb + i, h)),
        out_shape=jax.ShapeDtypeStruct((t, D), BF),
        compiler_params=_cp(("parallel", "parallel", "arbitrary")),
        name="diff_attention",
    )(*args)


def _shift_tokens(u, d, first, cols):
    n = u.shape[0]
    r = pltpu.roll(u, (-d) % n, axis=0)
    row = lax.broadcasted_iota(jnp.int32, (SUBLANES, u.shape[1]), 0)
    edge = (row < -d) if d < 0 else (row >= SUBLANES - d)
    want = 0 if d < 0 else cols - SUBLANES
    pieces, prev = [], 0
    for g in range(0, n, SUBLANES):
        if (first + g) % cols == want:
            if g > prev:
                pieces.append(r[prev:g])
            pieces.append(jnp.where(edge, 0.0, r[g:g + SUBLANES]))
            prev = g + SUBLANES
    if prev < n:
        pieces.append(r[prev:])
    return jnp.concatenate(pieces, axis=0) if len(pieces) > 1 else pieces[0]


def _dwconv_rows(u_ref, r0, n, cw, cb, cols, two_d):
    halo = cols if two_d else SUBLANES
    lo, hi = r0 - halo, r0 + n + halo
    zero = jnp.zeros((halo, u_ref.shape[1]), F32)
    ext = jnp.concatenate(([zero] if lo < 0 else []) + [u_ref[max(lo, 0):min(hi, TOK), :]]
                          + ([zero] if hi > TOK else []), axis=0)
    ul = _shift_tokens(ext, -1, lo, cols)
    ur = _shift_tokens(ext, 1, lo, cols)
    cw = cw.astype(BF)
    cb = cb.astype(BF)
    if halo % (2 * SUBLANES) == 0:
        ul, ext, ur = ul.astype(BF), ext.astype(BF), ur.astype(BF)
        pick = lambda x, a: x[a:a + n]
    else:
        pick = lambda x, a: x[a:a + n].astype(BF)

    def taps(k, a):
        return cw[k:k + 1] * pick(ul, a) + cw[k + 1:k + 2] * pick(ext, a) + cw[k + 2:k + 3] * pick(ur, a)

    out = taps(3, halo) + cb
    if two_d:
        out = out + taps(0, 0) + taps(6, 2 * halo)
    return out


def _ffn_kernel(*refs, cols, two_d, final):
    x_ref, m_ref, g_ref, wa_ref, wv_ref, cwa_ref, cwv_ref, cba_ref, cbv_ref, wdn_ref = refs[:10]
    rest = refs[10:]
    if final:
        fg_ref, rest = rest[0], rest[1:]
    o_ref, h_sc, acc_sc, ua0_sc, uv0_sc, ua1_sc, uv1_sc = rest
    j = pl.program_id(1)

    @pl.when(j == 0)
    def _():
        m = m_ref[...]
        h = _rms(x_ref[...], g_ref[...]) * (1.0 + m[4:5]) + m[3:4]
        h_sc[...] = h.astype(BF)
        acc_sc[...] = jnp.zeros_like(acc_sc)
        ua1_sc[...] = jnp.zeros_like(ua1_sc)
        uv1_sc[...] = jnp.zeros_like(uv1_sc)

    def step(ua_w, uv_w, ua_r, uv_r):
        wa = wa_ref[...].astype(BF)
        wv = wv_ref[...].astype(BF)
        wd = wdn_ref[...].astype(BF)
        cwa, cwv, cba, cbv = cwa_ref[...], cwv_ref[...], cba_ref[...], cbv_ref[...]
        keep = jnp.where(j > 0, 1.0, 0.0).astype(BF)
        nrow = FFN_ROWS_2D if two_d else FFN_ROWS_1D
        for r0 in range(0, TOK, nrow):
            rows = slice(r0, r0 + nrow)
            h = h_sc[rows, :]
            ua_w[rows, :] = _dot(h, wa)
            uv_w[rows, :] = _dot(h, wv)
            a = _dwconv_rows(ua_r, r0, nrow, cwa, cba, cols, two_d)
            v = _dwconv_rows(uv_r, r0, nrow, cwv, cbv, cols, two_d)
            act = a * _sigmoid(a) * (v * keep)
            acc_sc[rows, :] += _dot(act, wd)

    @pl.when(j % 2 == 0)
    def _():
        step(ua0_sc, uv0_sc, ua1_sc, uv1_sc)

    @pl.when(j % 2 == 1)
    def _():
        step(ua1_sc, uv1_sc, ua0_sc, uv0_sc)

    @pl.when(j == N_FF_TILES)
    def _():
        y = x_ref[...] + m_ref[5:6, :] * acc_sc[...]
        if final:
            y = _rms(y, fg_ref[...])
        o_ref[...] = y


def _ffn(x, mods, layer, row0, row_step, g2, wup, cw, cb, wdn, cols, two_d, final_g=None):
    nblk = x.shape[0] // TOK
    nt = N_FF_TILES
    up = lambda j: jnp.minimum(j, nt - 1)
    dn = lambda j: jnp.maximum(j - 1, 0)
    in_specs = [pl.BlockSpec((TOK, D), lambda i, j: (i, 0)),
                _mod_spec(layer, row0, row_step),
                pl.BlockSpec((1, D), lambda i, j: (0, 0)),
                pl.BlockSpec((None, D, FF_TILE), lambda i, j: (layer, 0, up(j))),
                pl.BlockSpec((None, D, FF_TILE), lambda i, j: (layer, 0, nt + up(j))),
                pl.BlockSpec((None, 9, FF_TILE), lambda i, j: (layer, 0, dn(j))),
                pl.BlockSpec((None, 9, FF_TILE), lambda i, j: (layer, 0, nt + dn(j))),
                pl.BlockSpec((None, 1, FF_TILE), lambda i, j: (layer, 0, dn(j))),
                pl.BlockSpec((None, 1, FF_TILE), lambda i, j: (layer, 0, nt + dn(j))),
                pl.BlockSpec((None, FF_TILE, D), lambda i, j: (layer, dn(j), 0))]
    args = [x, mods, g2, wup, wup, cw, cw, cb, cb, wdn]
    if final_g is not None:
        in_specs.append(pl.BlockSpec((1, D), lambda i, j: (0, 0)))
        args.append(final_g)
    return pl.pallas_call(
        functools.partial(_ffn_kernel, cols=cols, two_d=two_d, final=final_g is not None),
        grid=(nblk, N_FF_TILES + 1),
        in_specs=in_specs,
        out_specs=pl.BlockSpec((TOK, D), lambda i, j: (i, 0)),
        out_shape=jax.ShapeDtypeStruct(x.shape, F32),
        scratch_shapes=[pltpu.VMEM((TOK, D), BF), pltpu.VMEM((TOK, D), F32),
                        *[pltpu.VMEM((TOK, FF_TILE), F32) for _ in range(4)]],
        compiler_params=_cp(("parallel", "arbitrary")),
        name="conv_ffn",
    )(*args)


def kernel(x_prompt, x_sample, state_hgrn, cache_k, cache_v, c, c_ctx, ada_w, ada_b, norm1, norm2, final_norm,
           pool_w, pool_scale, hgrn_w_in, hgrn_lb, hgrn_norm, hgrn_w_out, diff_w_in, diff_lambda, diff_subln,
           diff_w_out, ffn_w_up, ffn_conv_w, ffn_conv_b, ffn_w_down):
    nbp, seq, _ = x_prompt.shape
    nbs, dseq, _ = x_sample.shape
    assert seq * (TOK // seq) == TOK and dseq == TOK and D_FF == ffn_w_down.shape[1]
    rows = dseq // GRID_W

    cond = jnp.concatenate([c_ctx[None, :], c, jnp.zeros((8 - 1 - nbs, D), F32)], axis=0)
    mods = _modulation(cond.T, ada_w, ada_b).reshape(DEPTH, 8, 6, D)

    groups = [
        dict(x=x_prompt.reshape(nbp * seq, D), row0=0, step=0, n_seq=seq, two_d=False, cols=seq,
             pool=_pool_counts(1, seq, TOK // seq)),
        dict(x=x_sample.reshape(nbs * dseq, D), row0=1, step=1, n_seq=dseq, two_d=True, cols=GRID_W,
             pool=_pool_counts(rows, GRID_W, 1)),
    ]
    conv_w = ffn_conv_w.reshape(DEPTH, 9, 2 * D_FF)
    conv_b = ffn_conv_b.reshape(DEPTH, 1, 2 * D_FF)
    hgrn_consts = _hgrn_masks()
    rope_tabs = _rope_tables()

    lbs = jax.nn.softmax(hgrn_lb.astype(F32), axis=1)
    lbs = jnp.cumsum(lbs, axis=1) - lbs[:, :1]

    new_state = new_k = new_v = None
    for i in range(DEPTH):
        kind, j = i % 3, i // 3
        g1 = norm1[i][None, :]
        g2 = norm2[i][None, :]
        for gi, gr in enumerate(groups):
            x, row0, step = gr["x"], gr["row0"], gr["step"]
            is_prompt = gi == 0
            if kind == 0:
                x = _pool_layer(x, mods, i, row0, step, g1, gr["pool"], pool_w[j], pool_scale[j][None, :],
                                gr["cols"], gr["two_d"])
            elif kind == 1:
                proj = _inproj(x, mods, i, row0, step, g1, hgrn_w_in[j])
                lb = lbs[:, i].reshape(2, HEADS, HD).transpose(1, 0, 2)
                ng = hgrn_norm[j][None, :]
                if is_prompt:
                    og, new_state = _hgrn_scan(proj, gr["n_seq"], lb, ng, hgrn_consts, HGRN_HEADS_PROMPT, True,
                                               None, True)
                else:
                    (og,) = _hgrn_scan(proj, gr["n_seq"], lb, ng, hgrn_consts, HGRN_HEADS_SAMPLE,
                                       HGRN_UNROLL_SAMPLE, state_hgrn[:, j], False)
                x = _outproj(og, hgrn_w_out[j], x, mods, i, row0, step)
            else:
                lam_init = 0.8 - 0.6 * math.exp(-0.3 * i)
                if is_prompt:
                    qkv = _inproj(x, mods, i, row0, step, g1, diff_w_in[j])
                    new_k, new_v = qkv[1], qkv[2]
                    cache = None
                else:
                    qkv = _inproj(x, mods, i, row0, step, g1, diff_w_in[j], rope_tabs)
                    cache = (cache_k[:, j].reshape(nbs, -1, D), cache_v[:, j].reshape(nbs, -1, D))
                og = _attention(qkv, gr["n_seq"], diff_lambda[j], diff_subln[j][None, :], lam_init,
                                ATT_HEADS_PROMPT if is_prompt else ATT_HEADS_SAMPLE, cache)
                x = _outproj(og, diff_w_out[j], x, mods, i, row0, step)
            fg = final_norm[None, :] if i == DEPTH - 1 else None
            gr["x"] = _ffn(x, mods, i, row0, step, g2, ffn_w_up, conv_w, conv_b, ffn_w_down,
                           gr["cols"], gr["two_d"], fg)

    y_prompt = groups[0]["x"].reshape(nbp, seq, D)
    y_sample = groups[1]["x"].reshape(nbs, dseq, D)
    new_state_hgrn = new_state[:, None]
    new_cache_k = new_k.reshape(nbp, 1, seq, HEADS, 2, DIFF_DH)
    new_cache_v = new_v.reshape(nbp, 1, seq, HEADS, 2 * DIFF_DH)
    return (y_prompt, y_sample, new_state_hgrn, new_cache_k, new_cache_v)
```

```python
import functools
import math

import numpy as np
import jax
import jax.numpy as jnp
from jax import lax
from jax.experimental import pallas as pl
from jax.experimental.pallas import tpu as pltpu

F32 = jnp.float32
BF = jnp.bfloat16

D = 1024
DEPTH = 4
GRID_W = 64
POOL_WINDOWS = (2, 4, 8, 16)
POOL_GROUP = D // len(POOL_WINDOWS)
HEADS = 8
HD = 128
DIFF_DH = 64
ROPE_BASE = 10000.0
D_FF = 2816
EPS = 1e-6

LANES = 128
SUBLANES = 8
TOK = 1024
FF_TILE = 256
FFN_ROWS_1D = 128
FFN_ROWS_2D = 256
N_FF_TILES = D_FF // FF_TILE
CHUNK = 128
HGRN_HEADS_PROMPT = 2
HGRN_HEADS_SAMPLE = 2
HGRN_UNROLL_SAMPLE = 1
ATT_QB = 256
ATT_HEADS_PROMPT = 8
ATT_HEADS_SAMPLE = 2
MOD_TN = 1536
MOD_LC = 768
VMEM_LIMIT = 56 * 1024 * 1024


def _cp(sem, vmem=VMEM_LIMIT):
    return pltpu.CompilerParams(dimension_semantics=sem, vmem_limit_bytes=vmem)


def _sigmoid(x):
    return 1.0 / (1.0 + jnp.exp(-x))


def _rms(x, g):
    return x * lax.rsqrt(jnp.mean(x * x, axis=-1, keepdims=True) + EPS) * g


def _dot(a, b):
    return jnp.dot(a, b, preferred_element_type=F32)


def _dot_nt(a, b):
    return lax.dot_general(a, b, (((1,), (1,)), ((), ())), preferred_element_type=F32)


def _dot_tn(a, b):
    return lax.dot_general(a, b, (((0,), (0,)), ((), ())), preferred_element_type=F32)


def _mod_kernel(ct_ref, w_ref, b_ref, o_ref, sb_ref):
    @pl.when((pl.program_id(0) == 0) & (pl.program_id(1) == 0))
    def _():
        ct = ct_ref[...]
        s = ct * _sigmoid(ct)
        for r in range(3):
            sb_ref[r] = jnp.broadcast_to(s[:, r:r + 1], (D, LANES))

    for lc in range(MOD_TN // MOD_LC):
        c0 = lc * MOD_LC

        def body(i, accs, c0=c0):
            k0 = pl.multiple_of(i * SUBLANES, SUBLANES)
            wv = w_ref[pl.ds(k0, SUBLANES), c0:c0 + MOD_LC]
            out = []
            for r in range(3):
                sv = sb_ref[r, pl.ds(k0, SUBLANES), :]
                out.append(accs[r] + wv * jnp.tile(sv, (1, MOD_LC // LANES)))
            return tuple(out)

        accs = lax.fori_loop(0, D // SUBLANES, body,
                             tuple(jnp.zeros((SUBLANES, MOD_LC), F32) for _ in range(3)), unroll=4)
        for r in range(3):
            o_ref[r:r + 1, c0:c0 + MOD_LC] = (jnp.sum(accs[r], axis=0, keepdims=True)
                                              + b_ref[:, c0:c0 + MOD_LC])
        o_ref[3:8, c0:c0 + MOD_LC] = jnp.zeros((5, MOD_LC), F32)


def _modulation(cond_t, ada_w, ada_b):
    return pl.pallas_call(
        _mod_kernel,
        grid=(DEPTH, 6 * D // MOD_TN),
        in_specs=[pl.BlockSpec((D, 8), lambda l, j: (0, 0)),
                  pl.BlockSpec((None, D, MOD_TN), lambda l, j: (l, 0, j)),
                  pl.BlockSpec((None, 1, MOD_TN), lambda l, j: (l, 0, j))],
        out_specs=pl.BlockSpec((None, 8, MOD_TN), lambda l, j: (l, 0, j)),
        out_shape=jax.ShapeDtypeStruct((DEPTH, 8, 6 * D), F32),
        scratch_shapes=[pltpu.VMEM((3, D, LANES), F32)],
        compiler_params=_cp(("arbitrary", "arbitrary")),
        name="modulation",
    )(cond_t, ada_w, ada_b.reshape(DEPTH, 1, 6 * D))


def _mod_spec(layer, row0, row_step):
    return pl.BlockSpec((None, None, 6, D), lambda i, *_: (layer, row0 + row_step * i, 0, 0))


def _shift_grid_rows(a, n):
    z = jnp.zeros((abs(n), a.shape[1]), a.dtype)
    if n > 0:
        return jnp.concatenate([a[n:], z], axis=0)
    return jnp.concatenate([z, a[:n]], axis=0)


def _box_sum(x, w, shift):
    fwd = bwd = x
    k = 1
    while k < w // 2:
        fwd = fwd + shift(fwd, k)
        bwd = bwd + shift(bwd, -k)
        k *= 2
    return shift(bwd, -1) + fwd


def _pool_kernel(x_ref, m_ref, g_ref, ic_ref, w_ref, sc_ref, o_ref, *, cols, two_d):
    x = x_ref[...]
    m = m_ref[...]
    h = _rms(x, g_ref[...]) * (1.0 + m[1:2]) + m[0:1]
    for g, w in enumerate(POOL_WINDOWS):
        sl = slice(g * POOL_GROUP, (g + 1) * POOL_GROUP)
        hg = h[:, sl]
        s = _box_sum(hg, w, lambda a, d: _shift_tokens(a, d, 0, cols))
        if two_d:
            s = _box_sum(s, w, lambda a, d: _shift_grid_rows(a, d * cols))
        ic = ic_ref[g]
        d = s * jnp.concatenate([ic, ic], axis=1) - hg
        y = _dot(d.astype(BF), w_ref[g].astype(BF)) * sc_ref[:, sl]
        o_ref[:, sl] = x[:, sl] + m[2:3, sl] * y


def _pool_counts(rows, cols, segs):
    n = rows * cols
    r = np.arange(n) // cols
    c = np.arange(n) % cols
    invs = []
    for w in POOL_WINDOWS:
        def size(pos, length):
            return np.clip(pos + w // 2, 0, length) - np.clip(pos - w // 2, 0, length)
        cnt = (size(np.arange(rows), rows)[r] * size(np.arange(cols), cols)[c]).astype(np.float64)
        invs.append(np.tile(1.0 / cnt, segs))
    return jnp.asarray(np.broadcast_to(np.stack(invs)[:, :, None], (len(POOL_WINDOWS), n * segs, LANES)),
                       dtype=F32)


def _pool_layer(x, mods, layer, row0, row_step, g1, inv, w, scale, cols, two_d):
    nblk = x.shape[0] // TOK
    const3 = lambda i: (0, 0, 0)
    return pl.pallas_call(
        functools.partial(_pool_kernel, cols=cols, two_d=two_d),
        grid=(nblk,),
        in_specs=[pl.BlockSpec((TOK, D), lambda i: (i, 0)),
                  _mod_spec(layer, row0, row_step),
                  pl.BlockSpec((1, D), lambda i: (0, 0)),
                  pl.BlockSpec((4, TOK, LANES), const3),
                  pl.BlockSpec((4, POOL_GROUP, POOL_GROUP), const3),
                  pl.BlockSpec((1, D), lambda i: (0, 0))],
        out_specs=pl.BlockSpec((TOK, D), lambda i: (i, 0)),
        out_shape=jax.ShapeDtypeStruct(x.shape, F32),
        compiler_params=_cp(("parallel",)),
        name="pool_mixer",
    )(x, mods, g1, inv, w, scale)


def _rope_tile(y, cos, sa, sb):
    outs = []
    for c in range(D // LANES):
        yc = y[:, c * LANES:(c + 1) * LANES]
        up = pltpu.roll(yc, LANES - 16, axis=1)
        dn = pltpu.roll(yc, 16, axis=1)
        outs.append(yc * cos + up * sa + dn * sb)
    return jnp.concatenate(outs, axis=1)


def _inproj_kernel(x_ref, m_ref, g_ref, w_ref, *rest, rope, kt_seq):
    rest = list(rest)
    if rope:
        cos_ref, sa_ref, sb_ref = rest[:3]
        rest = rest[3:]
    o_ref = rest.pop(0)
    kt_ref = rest.pop(0) if kt_seq else None
    h_sc, w_sc = rest
    j = pl.program_id(0)
    i = pl.program_id(1)

    @pl.when(j == 0)
    def _():
        m = m_ref[...]
        h = _rms(x_ref[...], g_ref[...]) * (1.0 + m[1:2]) + m[0:1]
        h_sc[i] = h.astype(BF)

    @pl.when(i == 0)
    def _():
        w_sc[...] = w_ref[...].astype(BF)

    y = _dot(h_sc[i], w_sc[...])
    if rope:
        @pl.when(j < 2)
        def _():
            o_ref[...] = _rope_tile(y, cos_ref[...], sa_ref[...], sb_ref[...])

        @pl.when(j >= 2)
        def _():
            o_ref[...] = y
    else:
        o_ref[...] = y

    if kt_seq:
        @pl.when(j == 1)
        def _():
            yt = y.T
            for b in range(TOK // kt_seq):
                kt_ref[b] = yt[:, b * kt_seq:(b + 1) * kt_seq]


def _inproj(x, mods, layer, row0, row_step, g1, w, rope_tabs=None, kt_seq=None):
    t = x.shape[0]
    nblk = t // TOK
    ntile = w.shape[1] // D
    out_specs = [pl.BlockSpec((None, TOK, D), lambda j, i: (j, i, 0))]
    out_shape = [jax.ShapeDtypeStruct((ntile, t, D), F32)]
    if kt_seq:
        out_specs.append(pl.BlockSpec(
            (TOK // kt_seq, D, kt_seq),
            lambda j, i: (jnp.where(j < 1, 0, jnp.where(j == 1, i, nblk - 1)), 0, 0)))
        out_shape.append(jax.ShapeDtypeStruct((t // kt_seq, D, kt_seq), F32))
    in_specs = [pl.BlockSpec((TOK, D), lambda j, i: (jnp.where(j == 0, i, nblk - 1), 0)),
                pl.BlockSpec((None, None, 6, D), lambda j, i: (layer, row0 + row_step * i, 0, 0)),
                pl.BlockSpec((1, D), lambda j, i: (0, 0)),
                pl.BlockSpec((D, D), lambda j, i: (0, j))]
    args = [x, mods, g1, w]
    if rope_tabs is not None:
        in_specs += [pl.BlockSpec((TOK, LANES), lambda j, i: (0, 0))] * 3
        args += list(rope_tabs)
    res = pl.pallas_call(
        functools.partial(_inproj_kernel, rope=rope_tabs is not None, kt_seq=kt_seq),
        grid=(ntile, nblk),
        in_specs=in_specs,
        out_specs=out_specs,
        out_shape=out_shape,
        scratch_shapes=[pltpu.VMEM((nblk, TOK, D), BF), pltpu.VMEM((D, D), BF)],
        compiler_params=_cp(("arbitrary", "arbitrary")),
        name="in_proj",
    )(*args)
    return res if kt_seq else res[0]


def _rope_tables():
    t = np.arange(TOK)
    half = DIFF_DH // 2
    nf = half // 2
    f32 = np.float32
    inv = np.power(f32(ROPE_BASE), -np.arange(nf, dtype=f32) / f32(nf)).astype(f32)
    lane = np.arange(LANES) % DIFF_DH
    use_col = lane >= half
    fidx = (lane % half) % nf
    second = (lane % half) >= nf
    pos = np.where(use_col[None, :], (t % GRID_W).astype(f32)[:, None], (t // GRID_W).astype(f32)[:, None])
    ang = (pos * inv[fidx][None, :]).astype(f32)
    cos, sin = np.cos(ang).astype(f32), np.sin(ang).astype(f32)
    sa = np.where(second[None, :], f32(0), -sin)
    sb = np.where(second[None, :], sin, f32(0))
    return jnp.asarray(cos), jnp.asarray(sa), jnp.asarray(sb)


HGRN_LEVELS = (1, 2, 4, 8, 16, 32, 64)


def _hgrn_masks():
    t = np.arange(CHUNK)[:, None]
    s = np.arange(CHUNK)[None, :]
    fwd = [(t == s)]
    for b in HGRN_LEVELS:
        fwd.append((t // (2 * b) == s // (2 * b)) & (t % (2 * b) >= b) & (s % (2 * b) < b))
    fwd = np.stack(fwd).astype(np.float32)
    bwd = np.transpose(fwd, (0, 2, 1))
    tri = np.stack([(s <= t), (s >= t)]).astype(np.float32)
    return jnp.asarray(fwd), jnp.asarray(bwd), jnp.asarray(tri, dtype=BF)


def _neg_abs(x):
    bits = lax.bitcast_convert_type(x, jnp.uint32) | jnp.uint32(0x80000000)
    return lax.bitcast_convert_type(bits, F32)


def _hgrn_chunks(chains):
    n = range(len(chains))
    q = [c[0] for c in chains]
    masks = [c[5] for c in chains]
    rev = [c[7] for c in chains]
    f = [c[3] + (1.0 - c[3]) * _sigmoid(c[2]) for c in chains]
    lf = [jnp.log2(x) for x in f]
    k = [1.0 - x for x in f]
    l1 = [x.astype(BF) for x in lf]
    r1 = [lf[i] - l1[i].astype(F32) for i in n]
    l2 = [x.astype(BF) for x in r1]
    l3 = [(r1[i] - l2[i].astype(F32)).astype(BF) for i in n]
    cum = [_dot(chains[i][4], l1[i]) + _dot(chains[i][4], l2[i]) + _dot(chains[i][4], l3[i]) for i in n]
    tot = [cum[i][0:1] if rev[i] else cum[i][CHUNK - 1:CHUNK] for i in n]
    qb = [x.astype(BF) for x in q]
    kb = [x.astype(BF) for x in k]

    def level(idx, e, s):
        eb = [x.astype(BF) for x in e]
        sc = [_dot_nt(qb[i] * eb[i], kb[i] * eb[i]) for i in n]
        return [s[i] + masks[i][idx] * sc[i] for i in n]

    s = [masks[i][0] * _dot_nt(qb[i], kb[i]) for i in n]

    pos = lax.broadcasted_iota(jnp.int32, (CHUNK, HD), 0)
    odd = (pos & 1) == 1
    s = level(1, [jnp.where(odd, 1.0, f[i]) if rev[i] else jnp.where(odd, f[i], 1.0) for i in n], s)

    p4 = pos & 3
    ex = []
    for i in n:
        lfn = pltpu.roll(lf[i], CHUNK - 1, axis=0)
        lfp = pltpu.roll(lf[i], 1, axis=0)
        if rev[i]:
            ex.append(jnp.where(p4 == 0, lf[i] + lfn,
                                jnp.where(p4 == 1, lf[i], jnp.where(p4 == 2, 0.0, lfp))))
        else:
            ex.append(jnp.where(p4 == 0, lfn,
                                jnp.where(p4 == 1, 0.0, jnp.where(p4 == 2, lf[i], lf[i] + lfp))))
    s = level(2, [jnp.exp2(x) for x in ex], s)

    for li, b in enumerate(HGRN_LEVELS[2:]):
        e = []
        for i in n:
            c3 = cum[i].reshape(CHUNK // (2 * b), 2 * b, HD)
            ridx = b if rev[i] else b - 1
            e.append(jnp.exp2(_neg_abs(c3 - c3[:, ridx:ridx + 1, :])).reshape(CHUNK, HD))
        s = level(3 + li, e, s)

    vb = [c[1].astype(BF) for c in chains]
    qh = [(q[i] * jnp.exp2(cum[i])).astype(BF) for i in n]
    kh = [(k[i] * jnp.exp2(tot[i] - cum[i])).astype(BF) for i in n]
    o = [_dot(s[i].astype(BF), vb[i]) + _dot_nt(qh[i], chains[i][6].astype(BF)) for i in n]
    st_new = [chains[i][6] * jnp.exp2(tot[i]) + _dot_tn(vb[i], kh[i]) for i in n]
    return list(zip(o, st_new))


def _hgrn_kernel(*refs, n_chunks, hb, unroll, has_s0, emit_state):
    q_ref, v_ref, g_ref, zf_ref, zb_ref, lb_ref, ng_ref, mf_ref, mb_ref, tri_ref = refs[:10]
    rest = refs[10:]
    if has_s0:
        s0_ref, rest = rest[0], rest[1:]
    o_ref, rest = rest[0], rest[1:]
    if emit_state:
        st_ref, rest = rest[0], rest[1:]
    of_sc, ob_sc, st_sc = rest

    for hh in range(hb):
        for d in range(2):
            st_sc[hh, d] = s0_ref[d, hh].T if has_s0 else jnp.zeros((HD, HD), F32)

    def body(i, carry):
        cf = pl.multiple_of(i * CHUNK, CHUNK)
        cb = pl.multiple_of((n_chunks - 1 - i) * CHUNK, CHUNK)
        chains = []
        for hh in range(hb):
            ln = slice(hh * HD, (hh + 1) * HD)
            chains.append((q_ref[pl.ds(cf, CHUNK), ln], v_ref[pl.ds(cf, CHUNK), ln], zf_ref[pl.ds(cf, CHUNK), ln],
                           lb_ref[hh, 0:1, :], tri_ref[0], mf_ref, st_sc[hh, 0], False))
            chains.append((q_ref[pl.ds(cb, CHUNK), ln], v_ref[pl.ds(cb, CHUNK), ln], zb_ref[pl.ds(cb, CHUNK), ln],
                           lb_ref[hh, 1:2, :], tri_ref[1], mb_ref, st_sc[hh, 1], True))
        res = _hgrn_chunks(chains)
        for hh in range(hb):
            ln = slice(hh * HD, (hh + 1) * HD)
            (o_f, stf), (o_b, stb) = res[2 * hh], res[2 * hh + 1]
            of_sc[pl.ds(cf, CHUNK), ln] = o_f
            ob_sc[pl.ds(cb, CHUNK), ln] = o_b
            st_sc[hh, 0] = stf
            st_sc[hh, 1] = stb
        return carry

    lax.fori_loop(0, n_chunks, body, 0, unroll=unroll)
    for hh in range(hb):
        ln = slice(hh * HD, (hh + 1) * HD)
        if emit_state:
            st_ref[0, hh] = st_sc[hh, 0].T
            st_ref[1, hh] = st_sc[hh, 1].T
        g = g_ref[:, ln]
        o_ref[:, ln] = (_rms(of_sc[:, ln] + ob_sc[:, ln], ng_ref[...]) * (g * _sigmoid(g))).astype(BF)


def _hgrn_scan(proj, n_seq, lb, ng, consts, hb, unroll, s0=None, emit_state=False):
    t = proj.shape[1]
    nb = t // n_seq
    mf, mb, tri = consts
    w = hb * HD

    def pspec(k):
        return pl.BlockSpec((None, n_seq, w), lambda b, h, k=k: (k, b, h))

    c3 = lambda b, h: (0, 0, 0)
    st_spec = pl.BlockSpec((None, 2, hb, HD, HD), lambda b, h: (b, 0, h, 0, 0))
    in_specs = [pspec(0), pspec(1), pspec(2), pspec(3), pspec(4),
                pl.BlockSpec((hb, 2, HD), lambda b, h: (h, 0, 0)),
                pl.BlockSpec((1, HD), lambda b, h: (0, 0)),
                pl.BlockSpec(mf.shape, c3), pl.BlockSpec(mb.shape, c3), pl.BlockSpec(tri.shape, c3)]
    args = [proj, proj, proj, proj, proj, lb, ng, mf, mb, tri]
    if s0 is not None:
        in_specs.append(st_spec)
        args.append(s0)
    out_specs = [pl.BlockSpec((n_seq, w), lambda b, h: (b, h))]
    out_shape = [jax.ShapeDtypeStruct((t, D), BF)]
    if emit_state:
        out_specs.append(st_spec)
        out_shape.append(jax.ShapeDtypeStruct((nb, 2, HEADS, HD, HD), F32))
    res = pl.pallas_call(
        functools.partial(_hgrn_kernel, n_chunks=n_seq // CHUNK, hb=hb, unroll=unroll, has_s0=s0 is not None,
                          emit_state=emit_state),
        grid=(nb, HEADS // hb),
        in_specs=in_specs,
        out_specs=out_specs,
        out_shape=out_shape,
        scratch_shapes=[pltpu.VMEM((n_seq, w), F32), pltpu.VMEM((n_seq, w), F32),
                        pltpu.VMEM((hb, 2, HD, HD), F32)],
        compiler_params=_cp(("parallel", "parallel")),
        name="hgrn_scan",
    )(*args)
    return res


def _attn_kernel(*refs, lam_init, hb, has_cache, kt):
    lam_ref, q_ref = refs[:2]
    if has_cache:
        kc_ref, vc_ref, kn_ref, vn_ref, sg_ref, o_ref = refs[2:]
        k_refs, v_refs = (kc_ref, kn_ref), (vc_ref, vn_ref)
    else:
        kn_ref, vn_ref, sg_ref, o_ref = refs[2:]
        k_refs, v_refs = (kn_ref,), (vn_ref,)

    lp = lam_ref[...]
    lam = (jnp.exp(jnp.sum(lp[0:1] * lp[1:2], axis=-1, keepdims=True))
           - jnp.exp(jnp.sum(lp[2:3] * lp[3:4], axis=-1, keepdims=True)) + lam_init)
    first = lax.broadcasted_iota(jnp.int32, (1, HD), 1) < DIFF_DH

    qscale = DIFF_DH ** -0.5 * math.log2(math.e)
    for hh in range(hb):
        ln = slice(hh * HD, (hh + 1) * HD)
        q = q_ref[:, ln] * qscale
        qs = (jnp.where(first, q, 0.0).astype(BF), jnp.where(first, 0.0, q).astype(BF))
        if kt:
            ks = [r[ln, :].astype(BF) for r in k_refs]
        else:
            ks = [r[:, ln].astype(BF) for r in k_refs]
        vs = [r[:, ln].astype(BF) for r in v_refs]
        outs = []
        for qm in qs:
            sc = [_dot(qm, kk) if kt else _dot_nt(qm, kk) for kk in ks]
            mx = functools.reduce(jnp.maximum, [jnp.max(s, axis=-1, keepdims=True) for s in sc])
            es = [jnp.exp2(s - mx) for s in sc]
            den = functools.reduce(jnp.add, [jnp.sum(e, axis=-1, keepdims=True) for e in es])
            pv = functools.reduce(jnp.add, [_dot(e.astype(BF), vv) for e, vv in zip(es, vs)])
            outs.append(pv * (1.0 / den))
        o = outs[0] - lam * outs[1]
        o_ref[:, ln] = (_rms(o, sg_ref[...]) * (1.0 - lam_init)).astype(BF)


def _attention(qkv, n_seq, lam_p, subln, lam_init, hb, cache=None, k_t=None):
    t = qkv.shape[1]
    nb = t // n_seq
    nqb = n_seq // ATT_QB
    w = hb * HD
    in_specs = [pl.BlockSpec((4, DIFF_DH), lambda b, h, i: (0, 0)),
                pl.BlockSpec((None, ATT_QB, w), lambda b, h, i: (0, b * nqb + i, h))]
    args = [lam_p, qkv]
    if cache is not None:
        past = cache[0].shape[1]
        in_specs += [pl.BlockSpec((None, past, w), lambda b, h, i: (b, 0, h))] * 2
        args += [cache[0], cache[1]]
    if k_t is not None:
        in_specs.append(pl.BlockSpec((None, w, n_seq), lambda b, h, i: (b, h, 0)))
    else:
        in_specs.append(pl.BlockSpec((None, n_seq, w), lambda b, h, i: (1, b, h)))
    in_specs += [pl.BlockSpec((None, n_seq, w), lambda b, h, i: (2, b, h)),
                 pl.BlockSpec((1, HD), lambda b, h, i: (0, 0))]
    args += [qkv if k_t is None else k_t, qkv, subln]
    return pl.pallas_call(
        functools.partial(_attn_kernel, lam_init=lam_init, hb=hb, has_cache=cache is not None,
                          kt=k_t is not None),
        grid=(nb, HEADS // hb, nqb),
        in_specs=in_specs,
        out_specs=pl.BlockSpec((ATT_QB, w), lambda b, h, i: (b * nqb + i, h)),
        out_shape=jax.ShapeDtypeStruct((t, D), BF),
        compiler_params=_cp(("parallel", "parallel", "arbitrary")),
        name="diff_attention",
    )(*args)


def _shift_tokens(u, d, first, cols):
    n = u.shape[0]
    r = pltpu.roll(u, (-d) % n, axis=0)
    row = lax.broadcasted_iota(jnp.int32, (SUBLANES, u.shape[1]), 0)
    edge = (row < -d) if d < 0 else (row >= SUBLANES - d)
    want = 0 if d < 0 else cols - SUBLANES
    pieces, prev = [], 0
    for g in range(0, n, SUBLANES):
        if (first + g) % cols == want:
            if g > prev:
                pieces.append(r[prev:g])
            pieces.append(jnp.where(edge, 0.0, r[g:g + SUBLANES]))
            prev = g + SUBLANES
    if prev < n:
        pieces.append(r[prev:])
    return jnp.concatenate(pieces, axis=0) if len(pieces) > 1 else pieces[0]


def _dwconv_rows(u_ref, r0, n, cw, cb, cols, two_d):
    halo = cols if two_d else SUBLANES
    lo, hi = r0 - halo, r0 + n + halo
    zero = jnp.zeros((halo, u_ref.shape[1]), F32)
    ext = jnp.concatenate(([zero] if lo < 0 else []) + [u_ref[max(lo, 0):min(hi, TOK), :]]
                          + ([zero] if hi > TOK else []), axis=0)
    ul = _shift_tokens(ext, -1, lo, cols)
    ur = _shift_tokens(ext, 1, lo, cols)
    cw = cw.astype(BF)
    cb = cb.astype(BF)
    if halo % (2 * SUBLANES) == 0:
        ul, ext, ur = ul.astype(BF), ext.astype(BF), ur.astype(BF)
        pick = lambda x, a: x[a:a + n]
    else:
        pick = lambda x, a: x[a:a + n].astype(BF)

    def taps(k, a):
        return cw[k:k + 1] * pick(ul, a) + cw[k + 1:k + 2] * pick(ext, a) + cw[k + 2:k + 3] * pick(ur, a)

    out = taps(3, halo) + cb
    if two_d:
        out = out + taps(0, 0) + taps(6, 2 * halo)
    return out


def _ffn_kernel(*refs, cols, two_d, final, mixer):
    x_ref, m_ref, g_ref, wa_ref, wv_ref, cwa_ref, cwv_ref, cba_ref, cbv_ref, wdn_ref = refs[:10]
    rest = refs[10:]
    if final:
        fg_ref, rest = rest[0], rest[1:]
    if mixer:
        og_ref, wo_ref, rest = rest[0], rest[1], rest[2:]
    o_ref, h_sc, acc_sc, ua0_sc, uv0_sc, ua1_sc, uv1_sc = rest
    j = pl.program_id(1)

    @pl.when(j == 0)
    def _():
        m = m_ref[...]
        x = x_ref[...]
        if mixer:
            x = x + m[2:3] * _dot(og_ref[...], wo_ref[...].astype(BF))
            o_ref[...] = x
        h = _rms(x, g_ref[...]) * (1.0 + m[4:5]) + m[3:4]
        h_sc[...] = h.astype(BF)
        acc_sc[...] = jnp.zeros_like(acc_sc)
        ua1_sc[...] = jnp.zeros_like(ua1_sc)
        uv1_sc[...] = jnp.zeros_like(uv1_sc)

    def step(ua_w, uv_w, ua_r, uv_r):
        wa = wa_ref[...].astype(BF)
        wv = wv_ref[...].astype(BF)
        wd = wdn_ref[...].astype(BF)
        cwa, cwv, cba, cbv = cwa_ref[...], cwv_ref[...], cba_ref[...], cbv_ref[...]
        keep = jnp.where(j > 0, 1.0, 0.0).astype(BF)
        nrow = FFN_ROWS_2D if two_d else FFN_ROWS_1D
        for r0 in range(0, TOK, nrow):
            rows = slice(r0, r0 + nrow)
            h = h_sc[rows, :]
            ua_w[rows, :] = _dot(h, wa)
            uv_w[rows, :] = _dot(h, wv)
            a = _dwconv_rows(ua_r, r0, nrow, cwa, cba, cols, two_d)
            v = _dwconv_rows(uv_r, r0, nrow, cwv, cbv, cols, two_d)
            act = a * _sigmoid(a) * (v * keep)
            acc_sc[rows, :] += _dot(act, wd)

    @pl.when(j % 2 == 0)
    def _():
        step(ua0_sc, uv0_sc, ua1_sc, uv1_sc)

    @pl.when(j % 2 == 1)
    def _():
        step(ua1_sc, uv1_sc, ua0_sc, uv0_sc)

    @pl.when(j == N_FF_TILES)
    def _():
        y = (o_ref[...] if mixer else x_ref[...]) + m_ref[5:6, :] * acc_sc[...]
        if final:
            y = _rms(y, fg_ref[...])
        o_ref[...] = y


def _ffn(x, mods, layer, row0, row_step, g2, wup, cw, cb, wdn, cols, two_d, final_g=None, mixer=None):
    nblk = x.shape[0] // TOK
    nt = N_FF_TILES
    up = lambda j: jnp.minimum(j, nt - 1)
    dn = lambda j: jnp.maximum(j - 1, 0)
    in_specs = [pl.BlockSpec((TOK, D), lambda i, j: (i, 0)),
                _mod_spec(layer, row0, row_step),
                pl.BlockSpec((1, D), lambda i, j: (0, 0)),
                pl.BlockSpec((None, D, FF_TILE), lambda i, j: (layer, 0, up(j))),
                pl.BlockSpec((None, D, FF_TILE), lambda i, j: (layer, 0, nt + up(j))),
                pl.BlockSpec((None, 9, FF_TILE), lambda i, j: (layer, 0, dn(j))),
                pl.BlockSpec((None, 9, FF_TILE), lambda i, j: (layer, 0, nt + dn(j))),
                pl.BlockSpec((None, 1, FF_TILE), lambda i, j: (layer, 0, dn(j))),
                pl.BlockSpec((None, 1, FF_TILE), lambda i, j: (layer, 0, nt + dn(j))),
                pl.BlockSpec((None, FF_TILE, D), lambda i, j: (layer, dn(j), 0))]
    args = [x, mods, g2, wup, wup, cw, cw, cb, cb, wdn]
    if final_g is not None:
        in_specs.append(pl.BlockSpec((1, D), lambda i, j: (0, 0)))
        args.append(final_g)
    if mixer is not None:
        in_specs += [pl.BlockSpec((TOK, D), lambda i, j: (i, 0)), pl.BlockSpec((D, D), lambda i, j: (0, 0))]
        args += list(mixer)
    return pl.pallas_call(
        functools.partial(_ffn_kernel, cols=cols, two_d=two_d, final=final_g is not None,
                          mixer=mixer is not None),
        grid=(nblk, N_FF_TILES + 1),
        in_specs=in_specs,
        out_specs=pl.BlockSpec((TOK, D), lambda i, j: (i, 0)),
        out_shape=jax.ShapeDtypeStruct(x.shape, F32),
        scratch_shapes=[pltpu.VMEM((TOK, D), BF), pltpu.VMEM((TOK, D), F32),
                        *[pltpu.VMEM((TOK, FF_TILE), F32) for _ in range(4)]],
        compiler_params=_cp(("parallel", "arbitrary")),
        name="conv_ffn",
    )(*args)


def kernel(x_prompt, x_sample, state_hgrn, cache_k, cache_v, c, c_ctx, ada_w, ada_b, norm1, norm2, final_norm,
           pool_w, pool_scale, hgrn_w_in, hgrn_lb, hgrn_norm, hgrn_w_out, diff_w_in, diff_lambda, diff_subln,
           diff_w_out, ffn_w_up, ffn_conv_w, ffn_conv_b, ffn_w_down):
    nbp, seq, _ = x_prompt.shape
    nbs, dseq, _ = x_sample.shape
    assert seq * (TOK // seq) == TOK and dseq == TOK and D_FF == ffn_w_down.shape[1]
    rows = dseq // GRID_W

    cond = jnp.concatenate([c_ctx[None, :], c, jnp.zeros((8 - 1 - nbs, D), F32)], axis=0)
    mods = _modulation(cond.T, ada_w, ada_b).reshape(DEPTH, 8, 6, D)

    groups = [
        dict(x=x_prompt.reshape(nbp * seq, D), row0=0, step=0, n_seq=seq, two_d=False, cols=seq,
             pool=_pool_counts(1, seq, TOK // seq)),
        dict(x=x_sample.reshape(nbs * dseq, D), row0=1, step=1, n_seq=dseq, two_d=True, cols=GRID_W,
             pool=_pool_counts(rows, GRID_W, 1)),
    ]
    conv_w = ffn_conv_w.reshape(DEPTH, 9, 2 * D_FF)
    conv_b = ffn_conv_b.reshape(DEPTH, 1, 2 * D_FF)
    hgrn_consts = _hgrn_masks()
    rope_tabs = _rope_tables()

    lbs = jax.nn.softmax(hgrn_lb.astype(F32), axis=1)
    lbs = jnp.cumsum(lbs, axis=1) - lbs[:, :1]

    new_state = new_k_t = new_v = None
    for i in range(DEPTH):
        kind, j = i % 3, i // 3
        g1 = norm1[i][None, :]
        g2 = norm2[i][None, :]
        for gi, gr in enumerate(groups):
            x, row0, step = gr["x"], gr["row0"], gr["step"]
            is_prompt = gi == 0
            mixer = None
            if kind == 0:
                x = _pool_layer(x, mods, i, row0, step, g1, gr["pool"], pool_w[j], pool_scale[j][None, :],
                                gr["cols"], gr["two_d"])
            elif kind == 1:
                proj = _inproj(x, mods, i, row0, step, g1, hgrn_w_in[j])
                lb = lbs[:, i].reshape(2, HEADS, HD).transpose(1, 0, 2)
                ng = hgrn_norm[j][None, :]
                if is_prompt:
                    og, new_state = _hgrn_scan(proj, gr["n_seq"], lb, ng, hgrn_consts, HGRN_HEADS_PROMPT, True,
                                               None, True)
                else:
                    (og,) = _hgrn_scan(proj, gr["n_seq"], lb, ng, hgrn_consts, HGRN_HEADS_SAMPLE,
                                       HGRN_UNROLL_SAMPLE, state_hgrn[:, j], False)
                mixer = (og, hgrn_w_out[j])
            else:
                lam_init = 0.8 - 0.6 * math.exp(-0.3 * i)
                if is_prompt:
                    qkv, new_kt = _inproj(x, mods, i, row0, step, g1, diff_w_in[j], kt_seq=seq)
                    new_v = qkv[2]
                    cache = None
                else:
                    qkv = _inproj(x, mods, i, row0, step, g1, diff_w_in[j], rope_tabs)
                    new_kt = None
                    cache = (cache_k[:, j].reshape(nbs, -1, D), cache_v[:, j].reshape(nbs, -1, D))
                og = _attention(qkv, gr["n_seq"], diff_lambda[j], diff_subln[j][None, :], lam_init,
                                ATT_HEADS_PROMPT if is_prompt else ATT_HEADS_SAMPLE, cache,
                                new_kt if is_prompt else None)
                if is_prompt:
                    new_k_t = new_kt
                mixer = (og, diff_w_out[j])
            fg = final_norm[None, :] if i == DEPTH - 1 else None
            gr["x"] = _ffn(x, mods, i, row0, step, g2, ffn_w_up, conv_w, conv_b, ffn_w_down,
                           gr["cols"], gr["two_d"], fg, mixer)

    y_prompt = groups[0]["x"].reshape(nbp, seq, D)
    y_sample = groups[1]["x"].reshape(nbs, dseq, D)
    new_state_hgrn = new_state[:, None]
    new_cache_k = new_k_t.reshape(nbp, HEADS, 2, DIFF_DH, seq).transpose(0, 4, 1, 2, 3)[:, None]
    new_cache_v = new_v.reshape(nbp, 1, seq, HEADS, 2 * DIFF_DH)
    return (y_prompt, y_sample, new_state_hgrn, new_cache_k, new_cache_v)
```

```python
import functools
import math

import numpy as np
import jax
import jax.numpy as jnp
from jax import lax
from jax.experimental import pallas as pl
from jax.experimental.pallas import tpu as pltpu

F32 = jnp.float32
BF = jnp.bfloat16

D = 1024
DEPTH = 4
GRID_W = 64
POOL_WINDOWS = (2, 4, 8, 16)
POOL_GROUP = D // len(POOL_WINDOWS)
HEADS = 8
HD = 128
DIFF_DH = 64
ROPE_BASE = 10000.0
D_FF = 2816
EPS = 1e-6

LANES = 128
SUBLANES = 8
TOK = 1024
FF_TILE = 256
FFN_ROWS_1D = 128
FFN_ROWS_2D = 256
N_FF_TILES = D_FF // FF_TILE
CHUNK = 128
HGRN_HEADS_PROMPT = 4
HGRN_HEADS_SAMPLE = 2
HGRN_UNROLL_SAMPLE = 2
ATT_QB = 256
ATT_HEADS_PROMPT = 8
ATT_HEADS_SAMPLE = 2
MOD_TN = 1536
MOD_LC = 768
VMEM_LIMIT = 56 * 1024 * 1024


def _cp(sem, vmem=VMEM_LIMIT):
    return pltpu.CompilerParams(dimension_semantics=sem, vmem_limit_bytes=vmem)


def _sigmoid(x):
    return 1.0 / (1.0 + jnp.exp(-x))


def _rms(x, g):
    return x * lax.rsqrt(jnp.mean(x * x, axis=-1, keepdims=True) + EPS) * g


def _dot(a, b):
    return jnp.dot(a, b, preferred_element_type=F32)


def _dot_nt(a, b):
    return lax.dot_general(a, b, (((1,), (1,)), ((), ())), preferred_element_type=F32)


def _dot_tn(a, b):
    return lax.dot_general(a, b, (((0,), (0,)), ((), ())), preferred_element_type=F32)


def _mod_kernel(ct_ref, w_ref, b_ref, o_ref, sb_ref):
    @pl.when((pl.program_id(0) == 0) & (pl.program_id(1) == 0))
    def _():
        ct = ct_ref[...]
        s = ct * _sigmoid(ct)
        for r in range(3):
            sb_ref[r] = jnp.broadcast_to(s[:, r:r + 1], (D, LANES))

    for lc in range(MOD_TN // MOD_LC):
        c0 = lc * MOD_LC

        def body(i, accs, c0=c0):
            k0 = pl.multiple_of(i * SUBLANES, SUBLANES)
            wv = w_ref[pl.ds(k0, SUBLANES), c0:c0 + MOD_LC]
            out = []
            for r in range(3):
                sv = sb_ref[r, pl.ds(k0, SUBLANES), :]
                out.append(accs[r] + wv * jnp.tile(sv, (1, MOD_LC // LANES)))
            return tuple(out)

        accs = lax.fori_loop(0, D // SUBLANES, body,
                             tuple(jnp.zeros((SUBLANES, MOD_LC), F32) for _ in range(3)), unroll=4)
        for r in range(3):
            o_ref[r:r + 1, c0:c0 + MOD_LC] = (jnp.sum(accs[r], axis=0, keepdims=True)
                                              + b_ref[:, c0:c0 + MOD_LC])
        o_ref[3:8, c0:c0 + MOD_LC] = jnp.zeros((5, MOD_LC), F32)


def _modulation(cond_t, ada_w, ada_b):
    return pl.pallas_call(
        _mod_kernel,
        grid=(DEPTH, 6 * D // MOD_TN),
        in_specs=[pl.BlockSpec((D, 8), lambda l, j: (0, 0)),
                  pl.BlockSpec((None, D, MOD_TN), lambda l, j: (l, 0, j)),
                  pl.BlockSpec((None, 1, MOD_TN), lambda l, j: (l, 0, j))],
        out_specs=pl.BlockSpec((None, 8, MOD_TN), lambda l, j: (l, 0, j)),
        out_shape=jax.ShapeDtypeStruct((DEPTH, 8, 6 * D), F32),
        scratch_shapes=[pltpu.VMEM((3, D, LANES), F32)],
        compiler_params=_cp(("arbitrary", "arbitrary")),
        name="modulation",
    )(cond_t, ada_w, ada_b.reshape(DEPTH, 1, 6 * D))


def _mod_spec(layer, row0, row_step):
    return pl.BlockSpec((None, None, 6, D), lambda i, *_: (layer, row0 + row_step * i, 0, 0))


def _shift_grid_rows(a, n):
    z = jnp.zeros((abs(n), a.shape[1]), a.dtype)
    if n > 0:
        return jnp.concatenate([a[n:], z], axis=0)
    return jnp.concatenate([z, a[:n]], axis=0)


def _box_sum(x, w, shift):
    fwd = bwd = x
    k = 1
    while k < w // 2:
        fwd = fwd + shift(fwd, k)
        bwd = bwd + shift(bwd, -k)
        k *= 2
    return shift(bwd, -1) + fwd


def _pool_kernel(x_ref, m_ref, g_ref, ic_ref, w_ref, sc_ref, o_ref, *, cols, two_d):
    x = x_ref[...]
    m = m_ref[...]
    h = _rms(x, g_ref[...]) * (1.0 + m[1:2]) + m[0:1]
    for g, w in enumerate(POOL_WINDOWS):
        sl = slice(g * POOL_GROUP, (g + 1) * POOL_GROUP)
        hg = h[:, sl]
        s = _box_sum(hg, w, lambda a, d: _shift_tokens(a, d, 0, cols))
        if two_d:
            s = _box_sum(s, w, lambda a, d: _shift_grid_rows(a, d * cols))
        ic = ic_ref[g]
        d = s * jnp.concatenate([ic, ic], axis=1) - hg
        y = _dot(d.astype(BF), w_ref[g].astype(BF)) * sc_ref[:, sl]
        o_ref[:, sl] = x[:, sl] + m[2:3, sl] * y


def _pool_counts(rows, cols, segs):
    n = rows * cols
    r = np.arange(n) // cols
    c = np.arange(n) % cols
    invs = []
    for w in POOL_WINDOWS:
        def size(pos, length):
            return np.clip(pos + w // 2, 0, length) - np.clip(pos - w // 2, 0, length)
        cnt = (size(np.arange(rows), rows)[r] * size(np.arange(cols), cols)[c]).astype(np.float64)
        invs.append(np.tile(1.0 / cnt, segs))
    return jnp.asarray(np.broadcast_to(np.stack(invs)[:, :, None], (len(POOL_WINDOWS), n * segs, LANES)),
                       dtype=F32)


def _pool_layer(x, mods, layer, row0, row_step, g1, inv, w, scale, cols, two_d):
    nblk = x.shape[0] // TOK
    const3 = lambda i: (0, 0, 0)
    return pl.pallas_call(
        functools.partial(_pool_kernel, cols=cols, two_d=two_d),
        grid=(nblk,),
        in_specs=[pl.BlockSpec((TOK, D), lambda i: (i, 0)),
                  _mod_spec(layer, row0, row_step),
                  pl.BlockSpec((1, D), lambda i: (0, 0)),
                  pl.BlockSpec((4, TOK, LANES), const3),
                  pl.BlockSpec((4, POOL_GROUP, POOL_GROUP), const3),
                  pl.BlockSpec((1, D), lambda i: (0, 0))],
        out_specs=pl.BlockSpec((TOK, D), lambda i: (i, 0)),
        out_shape=jax.ShapeDtypeStruct(x.shape, F32),
        compiler_params=_cp(("parallel",)),
        name="pool_mixer",
    )(x, mods, g1, inv, w, scale)


def _rope_tile(y, cos, sa, sb):
    outs = []
    for c in range(D // LANES):
        yc = y[:, c * LANES:(c + 1) * LANES]
        up = pltpu.roll(yc, LANES - 16, axis=1)
        dn = pltpu.roll(yc, 16, axis=1)
        outs.append(yc * cos + up * sa + dn * sb)
    return jnp.concatenate(outs, axis=1)


def _inproj_kernel(x_ref, m_ref, g_ref, w_ref, *rest, rope, kt_seq):
    rest = list(rest)
    if rope:
        cos_ref, sa_ref, sb_ref = rest[:3]
        rest = rest[3:]
    o_ref = rest.pop(0)
    kt_ref = rest.pop(0) if kt_seq else None
    h_sc, w_sc = rest
    j = pl.program_id(0)
    i = pl.program_id(1)

    @pl.when(j == 0)
    def _():
        m = m_ref[...]
        h = _rms(x_ref[...], g_ref[...]) * (1.0 + m[1:2]) + m[0:1]
        h_sc[i] = h.astype(BF)

    @pl.when(i == 0)
    def _():
        w_sc[...] = w_ref[...].astype(BF)

    y = _dot(h_sc[i], w_sc[...])
    if rope:
        @pl.when(j < 2)
        def _():
            o_ref[...] = _rope_tile(y, cos_ref[...], sa_ref[...], sb_ref[...])

        @pl.when(j >= 2)
        def _():
            o_ref[...] = y
    else:
        o_ref[...] = y

    if kt_seq:
        @pl.when(j == 1)
        def _():
            yt = y.T
            for b in range(TOK // kt_seq):
                kt_ref[b] = yt[:, b * kt_seq:(b + 1) * kt_seq]


def _inproj(x, mods, layer, row0, row_step, g1, w, rope_tabs=None, kt_seq=None):
    t = x.shape[0]
    nblk = t // TOK
    ntile = w.shape[1] // D
    out_specs = [pl.BlockSpec((None, TOK, D), lambda j, i: (j, i, 0))]
    out_shape = [jax.ShapeDtypeStruct((ntile, t, D), F32)]
    if kt_seq:
        out_specs.append(pl.BlockSpec(
            (TOK // kt_seq, D, kt_seq),
            lambda j, i: (jnp.where(j < 1, 0, jnp.where(j == 1, i, nblk - 1)), 0, 0)))
        out_shape.append(jax.ShapeDtypeStruct((t // kt_seq, D, kt_seq), F32))
    in_specs = [pl.BlockSpec((TOK, D), lambda j, i: (jnp.where(j == 0, i, nblk - 1), 0)),
                pl.BlockSpec((None, None, 6, D), lambda j, i: (layer, row0 + row_step * i, 0, 0)),
                pl.BlockSpec((1, D), lambda j, i: (0, 0)),
                pl.BlockSpec((D, D), lambda j, i: (0, j))]
    args = [x, mods, g1, w]
    if rope_tabs is not None:
        in_specs += [pl.BlockSpec((TOK, LANES), lambda j, i: (0, 0))] * 3
        args += list(rope_tabs)
    res = pl.pallas_call(
        functools.partial(_inproj_kernel, rope=rope_tabs is not None, kt_seq=kt_seq),
        grid=(ntile, nblk),
        in_specs=in_specs,
        out_specs=out_specs,
        out_shape=out_shape,
        scratch_shapes=[pltpu.VMEM((nblk, TOK, D), BF), pltpu.VMEM((D, D), BF)],
        compiler_params=_cp(("arbitrary", "arbitrary")),
        name="in_proj",
    )(*args)
    return res if kt_seq else res[0]


def _rope_tables():
    t = np.arange(TOK)
    half = DIFF_DH // 2
    nf = half // 2
    f32 = np.float32
    inv = np.power(f32(ROPE_BASE), -np.arange(nf, dtype=f32) / f32(nf)).astype(f32)
    lane = np.arange(LANES) % DIFF_DH
    use_col = lane >= half
    fidx = (lane % half) % nf
    second = (lane % half) >= nf
    pos = np.where(use_col[None, :], (t % GRID_W).astype(f32)[:, None], (t // GRID_W).astype(f32)[:, None])
    ang = (pos * inv[fidx][None, :]).astype(f32)
    cos, sin = np.cos(ang).astype(f32), np.sin(ang).astype(f32)
    sa = np.where(second[None, :], f32(0), -sin)
    sb = np.where(second[None, :], sin, f32(0))
    return jnp.asarray(cos), jnp.asarray(sa), jnp.asarray(sb)


HGRN_LEVELS = (1, 2, 4, 8, 16, 32, 64)


def _hgrn_masks():
    t = np.arange(CHUNK)[:, None]
    s = np.arange(CHUNK)[None, :]
    fwd = [(t == s)]
    for b in HGRN_LEVELS:
        fwd.append((t // (2 * b) == s // (2 * b)) & (t % (2 * b) >= b) & (s % (2 * b) < b))
    fwd = np.stack(fwd).astype(np.float32)
    bwd = np.transpose(fwd, (0, 2, 1))
    tri = np.stack([(s <= t), (s >= t)]).astype(np.float32)
    return jnp.asarray(fwd), jnp.asarray(bwd), jnp.asarray(tri, dtype=BF)


def _neg_abs(x):
    bits = lax.bitcast_convert_type(x, jnp.uint32) | jnp.uint32(0x80000000)
    return lax.bitcast_convert_type(bits, F32)


def _hgrn_chunks(chains):
    n = range(len(chains))
    q = [c[0] for c in chains]
    masks = [c[5] for c in chains]
    rev = [c[7] for c in chains]
    f = [c[3] + (1.0 - c[3]) * _sigmoid(c[2]) for c in chains]
    lf = [jnp.log2(x) for x in f]
    k = [1.0 - x for x in f]
    l1 = [x.astype(BF) for x in lf]
    r1 = [lf[i] - l1[i].astype(F32) for i in n]
    l2 = [x.astype(BF) for x in r1]
    l3 = [(r1[i] - l2[i].astype(F32)).astype(BF) for i in n]
    cum = [_dot(chains[i][4], l1[i]) + _dot(chains[i][4], l2[i]) + _dot(chains[i][4], l3[i]) for i in n]
    tot = [cum[i][0:1] if rev[i] else cum[i][CHUNK - 1:CHUNK] for i in n]
    qb = [x.astype(BF) for x in q]
    kb = [x.astype(BF) for x in k]

    def level(idx, e, s):
        eb = [x.astype(BF) for x in e]
        sc = [_dot_nt(qb[i] * eb[i], kb[i] * eb[i]) for i in n]
        return [s[i] + masks[i][idx] * sc[i] for i in n]

    s = [masks[i][0] * _dot_nt(qb[i], kb[i]) for i in n]

    pos = lax.broadcasted_iota(jnp.int32, (CHUNK, HD), 0)
    odd = (pos & 1) == 1
    s = level(1, [jnp.where(odd, 1.0, f[i]) if rev[i] else jnp.where(odd, f[i], 1.0) for i in n], s)

    p4 = pos & 3
    ex = []
    for i in n:
        lfn = pltpu.roll(lf[i], CHUNK - 1, axis=0)
        lfp = pltpu.roll(lf[i], 1, axis=0)
        if rev[i]:
            ex.append(jnp.where(p4 == 0, lf[i] + lfn,
                                jnp.where(p4 == 1, lf[i], jnp.where(p4 == 2, 0.0, lfp))))
        else:
            ex.append(jnp.where(p4 == 0, lfn,
                                jnp.where(p4 == 1, 0.0, jnp.where(p4 == 2, lf[i], lf[i] + lfp))))
    s = level(2, [jnp.exp2(x) for x in ex], s)

    for li, b in enumerate(HGRN_LEVELS[2:]):
        e = []
        for i in n:
            c3 = cum[i].reshape(CHUNK // (2 * b), 2 * b, HD)
            ridx = b if rev[i] else b - 1
            e.append(jnp.exp2(_neg_abs(c3 - c3[:, ridx:ridx + 1, :])).reshape(CHUNK, HD))
        s = level(3 + li, e, s)

    vb = [c[1].astype(BF) for c in chains]
    qh = [(q[i] * jnp.exp2(cum[i])).astype(BF) for i in n]
    kh = [(k[i] * jnp.exp2(tot[i] - cum[i])).astype(BF) for i in n]
    o = [_dot(s[i].astype(BF), vb[i]) + _dot_nt(qh[i], chains[i][6].astype(BF)) for i in n]
    st_new = [chains[i][6] * jnp.exp2(tot[i]) + _dot_tn(vb[i], kh[i]) for i in n]
    return list(zip(o, st_new))


def _hgrn_kernel(*refs, n_chunks, hb, unroll, has_s0, emit_state):
    q_ref, v_ref, g_ref, zf_ref, zb_ref, lb_ref, ng_ref, mf_ref, mb_ref, tri_ref = refs[:10]
    rest = refs[10:]
    if has_s0:
        s0_ref, rest = rest[0], rest[1:]
    o_ref, rest = rest[0], rest[1:]
    if emit_state:
        st_ref, rest = rest[0], rest[1:]
    of_sc, ob_sc, st_sc = rest

    for hh in range(hb):
        for d in range(2):
            st_sc[hh, d] = s0_ref[d, hh].T if has_s0 else jnp.zeros((HD, HD), F32)

    def body(i, carry):
        cf = pl.multiple_of(i * CHUNK, CHUNK)
        cb = pl.multiple_of((n_chunks - 1 - i) * CHUNK, CHUNK)
        chains = []
        for hh in range(hb):
            ln = slice(hh * HD, (hh + 1) * HD)
            chains.append((q_ref[pl.ds(cf, CHUNK), ln], v_ref[pl.ds(cf, CHUNK), ln], zf_ref[pl.ds(cf, CHUNK), ln],
                           lb_ref[hh, 0:1, :], tri_ref[0], mf_ref, st_sc[hh, 0], False))
            chains.append((q_ref[pl.ds(cb, CHUNK), ln], v_ref[pl.ds(cb, CHUNK), ln], zb_ref[pl.ds(cb, CHUNK), ln],
                           lb_ref[hh, 1:2, :], tri_ref[1], mb_ref, st_sc[hh, 1], True))
        res = _hgrn_chunks(chains)
        for hh in range(hb):
            ln = slice(hh * HD, (hh + 1) * HD)
            (o_f, stf), (o_b, stb) = res[2 * hh], res[2 * hh + 1]
            of_sc[pl.ds(cf, CHUNK), ln] = o_f
            ob_sc[pl.ds(cb, CHUNK), ln] = o_b
            st_sc[hh, 0] = stf
            st_sc[hh, 1] = stb
        return carry

    lax.fori_loop(0, n_chunks, body, 0, unroll=unroll)
    for hh in range(hb):
        ln = slice(hh * HD, (hh + 1) * HD)
        if emit_state:
            st_ref[0, hh] = st_sc[hh, 0].T
            st_ref[1, hh] = st_sc[hh, 1].T
        g = g_ref[:, ln]
        o_ref[:, ln] = (_rms(of_sc[:, ln] + ob_sc[:, ln], ng_ref[...]) * (g * _sigmoid(g))).astype(BF)


def _hgrn_scan(proj, n_seq, lb, ng, consts, hb, unroll, s0=None, emit_state=False):
    t = proj.shape[1]
    nb = t // n_seq
    mf, mb, tri = consts
    w = hb * HD

    def pspec(k):
        return pl.BlockSpec((None, n_seq, w), lambda b, h, k=k: (k, b, h))

    c3 = lambda b, h: (0, 0, 0)
    st_spec = pl.BlockSpec((None, 2, hb, HD, HD), lambda b, h: (b, 0, h, 0, 0))
    in_specs = [pspec(0), pspec(1), pspec(2), pspec(3), pspec(4),
                pl.BlockSpec((hb, 2, HD), lambda b, h: (h, 0, 0)),
                pl.BlockSpec((1, HD), lambda b, h: (0, 0)),
                pl.BlockSpec(mf.shape, c3), pl.BlockSpec(mb.shape, c3), pl.BlockSpec(tri.shape, c3)]
    args = [proj, proj, proj, proj, proj, lb, ng, mf, mb, tri]
    if s0 is not None:
        in_specs.append(st_spec)
        args.append(s0)
    out_specs = [pl.BlockSpec((n_seq, w), lambda b, h: (b, h))]
    out_shape = [jax.ShapeDtypeStruct((t, D), BF)]
    if emit_state:
        out_specs.append(st_spec)
        out_shape.append(jax.ShapeDtypeStruct((nb, 2, HEADS, HD, HD), F32))
    res = pl.pallas_call(
        functools.partial(_hgrn_kernel, n_chunks=n_seq // CHUNK, hb=hb, unroll=unroll, has_s0=s0 is not None,
                          emit_state=emit_state),
        grid=(nb, HEADS // hb),
        in_specs=in_specs,
        out_specs=out_specs,
        out_shape=out_shape,
        scratch_shapes=[pltpu.VMEM((n_seq, w), F32), pltpu.VMEM((n_seq, w), F32),
                        pltpu.VMEM((hb, 2, HD, HD), F32)],
        compiler_params=_cp(("parallel", "parallel")),
        name="hgrn_scan",
    )(*args)
    return res


def _attn_kernel(*refs, lam_init, hb, has_cache, kt):
    lam_ref, q_ref = refs[:2]
    if has_cache:
        kc_ref, vc_ref, kn_ref, vn_ref, sg_ref, o_ref = refs[2:]
        k_refs, v_refs = (kc_ref, kn_ref), (vc_ref, vn_ref)
    else:
        kn_ref, vn_ref, sg_ref, o_ref = refs[2:]
        k_refs, v_refs = (kn_ref,), (vn_ref,)

    lp = lam_ref[...]
    lam = (jnp.exp(jnp.sum(lp[0:1] * lp[1:2], axis=-1, keepdims=True))
           - jnp.exp(jnp.sum(lp[2:3] * lp[3:4], axis=-1, keepdims=True)) + lam_init)
    first = lax.broadcasted_iota(jnp.int32, (1, HD), 1) < DIFF_DH

    qscale = DIFF_DH ** -0.5 * math.log2(math.e)
    for hh in range(hb):
        ln = slice(hh * HD, (hh + 1) * HD)
        q = q_ref[:, ln] * qscale
        qs = (jnp.where(first, q, 0.0).astype(BF), jnp.where(first, 0.0, q).astype(BF))
        if kt:
            ks = [r[ln, :].astype(BF) for r in k_refs]
        else:
            ks = [r[:, ln].astype(BF) for r in k_refs]
        vs = [r[:, ln].astype(BF) for r in v_refs]
        outs = []
        for qm in qs:
            sc = [_dot(qm, kk) if kt else _dot_nt(qm, kk) for kk in ks]
            mx = functools.reduce(jnp.maximum, [jnp.max(s, axis=-1, keepdims=True) for s in sc])
            es = [jnp.exp2(s - mx) for s in sc]
            den = functools.reduce(jnp.add, [jnp.sum(e, axis=-1, keepdims=True) for e in es])
            pv = functools.reduce(jnp.add, [_dot(e.astype(BF), vv) for e, vv in zip(es, vs)])
            outs.append(pv * (1.0 / den))
        o = outs[0] - lam * outs[1]
        o_ref[:, ln] = (_rms(o, sg_ref[...]) * (1.0 - lam_init)).astype(BF)


def _attention(qkv, n_seq, lam_p, subln, lam_init, hb, cache=None, k_t=None):
    t = qkv.shape[1]
    nb = t // n_seq
    nqb = n_seq // ATT_QB
    w = hb * HD
    in_specs = [pl.BlockSpec((4, DIFF_DH), lambda b, h, i: (0, 0)),
                pl.BlockSpec((None, ATT_QB, w), lambda b, h, i: (0, b * nqb + i, h))]
    args = [lam_p, qkv]
    if cache is not None:
        past = cache[0].shape[1]
        in_specs += [pl.BlockSpec((None, past, w), lambda b, h, i: (b, 0, h))] * 2
        args += [cache[0], cache[1]]
    if k_t is not None:
        in_specs.append(pl.BlockSpec((None, w, n_seq), lambda b, h, i: (b, h, 0)))
    else:
        in_specs.append(pl.BlockSpec((None, n_seq, w), lambda b, h, i: (1, b, h)))
    in_specs += [pl.BlockSpec((None, n_seq, w), lambda b, h, i: (2, b, h)),
                 pl.BlockSpec((1, HD), lambda b, h, i: (0, 0))]
    args += [qkv if k_t is None else k_t, qkv, subln]
    return pl.pallas_call(
        functools.partial(_attn_kernel, lam_init=lam_init, hb=hb, has_cache=cache is not None,
                          kt=k_t is not None),
        grid=(nb, HEADS // hb, nqb),
        in_specs=in_specs,
        out_specs=pl.BlockSpec((ATT_QB, w), lambda b, h, i: (b * nqb + i, h)),
        out_shape=jax.ShapeDtypeStruct((t, D), BF),
        compiler_params=_cp(("parallel", "parallel", "arbitrary")),
        name="diff_attention",
    )(*args)


def _shift_tokens(u, d, first, cols):
    n = u.shape[0]
    r = pltpu.roll(u, (-d) % n, axis=0)
    row = lax.broadcasted_iota(jnp.int32, (SUBLANES, u.shape[1]), 0)
    edge = (row < -d) if d < 0 else (row >= SUBLANES - d)
    want = 0 if d < 0 else cols - SUBLANES
    pieces, prev = [], 0
    for g in range(0, n, SUBLANES):
        if (first + g) % cols == want:
            if g > prev:
                pieces.append(r[prev:g])
            pieces.append(jnp.where(edge, 0.0, r[g:g + SUBLANES]))
            prev = g + SUBLANES
    if prev < n:
        pieces.append(r[prev:])
    return jnp.concatenate(pieces, axis=0) if len(pieces) > 1 else pieces[0]


def _dwconv_rows(u_ref, r0, n, cw, cb, cols, two_d):
    halo = cols if two_d else SUBLANES
    lo, hi = r0 - halo, r0 + n + halo
    zero = jnp.zeros((halo, u_ref.shape[1]), F32)
    ext = jnp.concatenate(([zero] if lo < 0 else []) + [u_ref[max(lo, 0):min(hi, TOK), :]]
                          + ([zero] if hi > TOK else []), axis=0)
    ul = _shift_tokens(ext, -1, lo, cols)
    ur = _shift_tokens(ext, 1, lo, cols)
    cw = cw.astype(BF)
    cb = cb.astype(BF)
    if halo % (2 * SUBLANES) == 0:
        ul, ext, ur = ul.astype(BF), ext.astype(BF), ur.astype(BF)
        pick = lambda x, a: x[a:a + n]
    else:
        pick = lambda x, a: x[a:a + n].astype(BF)

    def taps(k, a):
        return cw[k:k + 1] * pick(ul, a) + cw[k + 1:k + 2] * pick(ext, a) + cw[k + 2:k + 3] * pick(ur, a)

    out = taps(3, halo) + cb
    if two_d:
        out = out + taps(0, 0) + taps(6, 2 * halo)
    return out


def _ffn_kernel(*refs, cols, two_d, final, mixer):
    x_ref, m_ref, g_ref, wa_ref, wv_ref, cwa_ref, cwv_ref, cba_ref, cbv_ref, wdn_ref = refs[:10]
    rest = refs[10:]
    if final:
        fg_ref, rest = rest[0], rest[1:]
    if mixer:
        og_ref, wo_ref, rest = rest[0], rest[1], rest[2:]
    o_ref, h_sc, acc_sc, ua0_sc, uv0_sc, ua1_sc, uv1_sc = rest
    j = pl.program_id(1)

    @pl.when(j == 0)
    def _():
        m = m_ref[...]
        x = x_ref[...]
        if mixer:
            x = x + m[2:3] * _dot(og_ref[...], wo_ref[...].astype(BF))
            o_ref[...] = x
        h = _rms(x, g_ref[...]) * (1.0 + m[4:5]) + m[3:4]
        h_sc[...] = h.astype(BF)
        acc_sc[...] = jnp.zeros_like(acc_sc)
        ua1_sc[...] = jnp.zeros_like(ua1_sc)
        uv1_sc[...] = jnp.zeros_like(uv1_sc)

    def step(ua_w, uv_w, ua_r, uv_r):
        wa = wa_ref[...].astype(BF)
        wv = wv_ref[...].astype(BF)
        wd = wdn_ref[...].astype(BF)
        cwa, cwv, cba, cbv = cwa_ref[...], cwv_ref[...], cba_ref[...], cbv_ref[...]
        keep = jnp.where(j > 0, 1.0, 0.0).astype(BF)
        nrow = FFN_ROWS_2D if two_d else FFN_ROWS_1D
        for r0 in range(0, TOK, nrow):
            rows = slice(r0, r0 + nrow)
            h = h_sc[rows, :]
            ua_w[rows, :] = _dot(h, wa)
            uv_w[rows, :] = _dot(h, wv)
            a = _dwconv_rows(ua_r, r0, nrow, cwa, cba, cols, two_d)
            v = _dwconv_rows(uv_r, r0, nrow, cwv, cbv, cols, two_d)
            act = a * _sigmoid(a) * (v * keep)
            acc_sc[rows, :] += _dot(act, wd)

    @pl.when(j % 2 == 0)
    def _():
        step(ua0_sc, uv0_sc, ua1_sc, uv1_sc)

    @pl.when(j % 2 == 1)
    def _():
        step(ua1_sc, uv1_sc, ua0_sc, uv0_sc)

    @pl.when(j == N_FF_TILES)
    def _():
        y = (o_ref[...] if mixer else x_ref[...]) + m_ref[5:6, :] * acc_sc[...]
        if final:
            y = _rms(y, fg_ref[...])
        o_ref[...] = y


def _ffn(x, mods, layer, row0, row_step, g2, wup, cw, cb, wdn, cols, two_d, final_g=None, mixer=None):
    nblk = x.shape[0] // TOK
    nt = N_FF_TILES
    up = lambda j: jnp.minimum(j, nt - 1)
    dn = lambda j: jnp.maximum(j - 1, 0)
    in_specs = [pl.BlockSpec((TOK, D), lambda i, j: (i, 0)),
                _mod_spec(layer, row0, row_step),
                pl.BlockSpec((1, D), lambda i, j: (0, 0)),
                pl.BlockSpec((None, D, FF_TILE), lambda i, j: (layer, 0, up(j))),
                pl.BlockSpec((None, D, FF_TILE), lambda i, j: (layer, 0, nt + up(j))),
                pl.BlockSpec((None, 9, FF_TILE), lambda i, j: (layer, 0, dn(j))),
                pl.BlockSpec((None, 9, FF_TILE), lambda i, j: (layer, 0, nt + dn(j))),
                pl.BlockSpec((None, 1, FF_TILE), lambda i, j: (layer, 0, dn(j))),
                pl.BlockSpec((None, 1, FF_TILE), lambda i, j: (layer, 0, nt + dn(j))),
                pl.BlockSpec((None, FF_TILE, D), lambda i, j: (layer, dn(j), 0))]
    args = [x, mods, g2, wup, wup, cw, cw, cb, cb, wdn]
    if final_g is not None:
        in_specs.append(pl.BlockSpec((1, D), lambda i, j: (0, 0)))
        args.append(final_g)
    if mixer is not None:
        in_specs += [pl.BlockSpec((TOK, D), lambda i, j: (i, 0)), pl.BlockSpec((D, D), lambda i, j: (0, 0))]
        args += list(mixer)
    return pl.pallas_call(
        functools.partial(_ffn_kernel, cols=cols, two_d=two_d, final=final_g is not None,
                          mixer=mixer is not None),
        grid=(nblk, N_FF_TILES + 1),
        in_specs=in_specs,
        out_specs=pl.BlockSpec((TOK, D), lambda i, j: (i, 0)),
        out_shape=jax.ShapeDtypeStruct(x.shape, F32),
        scratch_shapes=[pltpu.VMEM((TOK, D), BF), pltpu.VMEM((TOK, D), F32),
                        *[pltpu.VMEM((TOK, FF_TILE), F32) for _ in range(4)]],
        compiler_params=_cp(("parallel", "arbitrary")),
        name="conv_ffn",
    )(*args)


def kernel(x_prompt, x_sample, state_hgrn, cache_k, cache_v, c, c_ctx, ada_w, ada_b, norm1, norm2, final_norm,
           pool_w, pool_scale, hgrn_w_in, hgrn_lb, hgrn_norm, hgrn_w_out, diff_w_in, diff_lambda, diff_subln,
           diff_w_out, ffn_w_up, ffn_conv_w, ffn_conv_b, ffn_w_down):
    nbp, seq, _ = x_prompt.shape
    nbs, dseq, _ = x_sample.shape
    assert seq * (TOK // seq) == TOK and dseq == TOK and D_FF == ffn_w_down.shape[1]
    rows = dseq // GRID_W

    cond = jnp.concatenate([c_ctx[None, :], c, jnp.zeros((8 - 1 - nbs, D), F32)], axis=0)
    mods = _modulation(cond.T, ada_w, ada_b).reshape(DEPTH, 8, 6, D)

    groups = [
        dict(x=x_prompt.reshape(nbp * seq, D), row0=0, step=0, n_seq=seq, two_d=False, cols=seq,
             pool=_pool_counts(1, seq, TOK // seq)),
        dict(x=x_sample.reshape(nbs * dseq, D), row0=1, step=1, n_seq=dseq, two_d=True, cols=GRID_W,
             pool=_pool_counts(rows, GRID_W, 1)),
    ]
    conv_w = ffn_conv_w.reshape(DEPTH, 9, 2 * D_FF)
    conv_b = ffn_conv_b.reshape(DEPTH, 1, 2 * D_FF)
    hgrn_consts = _hgrn_masks()
    rope_tabs = _rope_tables()

    lbs = jax.nn.softmax(hgrn_lb.astype(F32), axis=1)
    lbs = jnp.cumsum(lbs, axis=1) - lbs[:, :1]

    new_state = new_k_t = new_v = None
    for i in range(DEPTH):
        kind, j = i % 3, i // 3
        g1 = norm1[i][None, :]
        g2 = norm2[i][None, :]
        for gi, gr in enumerate(groups):
            x, row0, step = gr["x"], gr["row0"], gr["step"]
            is_prompt = gi == 0
            mixer = None
            if kind == 0:
                x = _pool_layer(x, mods, i, row0, step, g1, gr["pool"], pool_w[j], pool_scale[j][None, :],
                                gr["cols"], gr["two_d"])
            elif kind == 1:
                proj = _inproj(x, mods, i, row0, step, g1, hgrn_w_in[j])
                lb = lbs[:, i].reshape(2, HEADS, HD).transpose(1, 0, 2)
                ng = hgrn_norm[j][None, :]
                if is_prompt:
                    og, new_state = _hgrn_scan(proj, gr["n_seq"], lb, ng, hgrn_consts, HGRN_HEADS_PROMPT, True,
                                               None, True)
                else:
                    (og,) = _hgrn_scan(proj, gr["n_seq"], lb, ng, hgrn_consts, HGRN_HEADS_SAMPLE,
                                       HGRN_UNROLL_SAMPLE, state_hgrn[:, j], False)
                mixer = (og, hgrn_w_out[j])
            else:
                lam_init = 0.8 - 0.6 * math.exp(-0.3 * i)
                if is_prompt:
                    qkv, new_kt = _inproj(x, mods, i, row0, step, g1, diff_w_in[j], kt_seq=seq)
                    new_v = qkv[2]
                    cache = None
                else:
                    qkv = _inproj(x, mods, i, row0, step, g1, diff_w_in[j], rope_tabs)
                    new_kt = None
                    cache = (cache_k[:, j].reshape(nbs, -1, D), cache_v[:, j].reshape(nbs, -1, D))
                og = _attention(qkv, gr["n_seq"], diff_lambda[j], diff_subln[j][None, :], lam_init,
                                ATT_HEADS_PROMPT if is_prompt else ATT_HEADS_SAMPLE, cache,
                                new_kt if is_prompt else None)
                if is_prompt:
                    new_k_t = new_kt
                mixer = (og, diff_w_out[j])
            fg = final_norm[None, :] if i == DEPTH - 1 else None
            gr["x"] = _ffn(x, mods, i, row0, step, g2, ffn_w_up, conv_w, conv_b, ffn_w_down,
                           gr["cols"], gr["two_d"], fg, mixer)

    y_prompt = groups[0]["x"].reshape(nbp, seq, D)
    y_sample = groups[1]["x"].reshape(nbs, dseq, D)
    new_state_hgrn = new_state[:, None]
    new_cache_k = new_k_t.reshape(nbp, HEADS, 2, DIFF_DH, seq).transpose(0, 4, 1, 2, 3)[:, None]
    new_cache_v = new_v.reshape(nbp, 1, seq, HEADS, 2 * DIFF_DH)
    return (y_prompt, y_sample, new_state_hgrn, new_cache_k, new_cache_v)
```
